```python
import jax, jax.numpy as jnp
from jax import lax
import numpy as np

D_MODEL = 2048
BATCH = 8
SEQ = 2048
DEPTH = 2

N_META = 16
N_MIXERS = 2
EPS = 1e-6

A_INNER = 2 * D_MODEL
A_HEADS = 8
A_HEAD_IN = A_INNER // A_HEADS
A_DV = A_HEAD_IN
A_DQK = A_HEAD_IN // 2
A_CONV = 4
A_CHUNK = 64

B_HEAD_DIM = 128
B_HEADS = D_MODEL // B_HEAD_DIM
B_INNER = B_HEADS * B_HEAD_DIM
B_QBLOCK = 128

N_A = (DEPTH + 1) // 2
N_B = DEPTH // 2

kernel_name = 'hybrid_mlstm_fox_meta'


def _rmsnorm(x, g):
    xf = x.astype(jnp.float32)
    y = xf * lax.rsqrt(jnp.mean(xf * xf, axis=-1, keepdims=True) + EPS)
    return (y * g.astype(jnp.float32)).astype(x.dtype)


def _causal_conv(x, w, b):
    y = lax.conv_general_dilated(x, w[:, None, :].astype(x.dtype), window_strides=(1,),
                                 padding=[(A_CONV - 1, 0)],
                                 dimension_numbers=('NWC', 'WIO', 'NWC'),
                                 feature_group_count=x.shape[-1])
    return y + b.astype(x.dtype)


def _mlstm_chunk(carry, inp):
    C, n, m = carry
    q, k, v, ig, lf = inp
    L = q.shape[-2]
    b = jnp.cumsum(lf, axis=-1)
    causal = jnp.tril(jnp.ones((L, L), dtype=bool))
    dmat = jnp.where(causal, b[..., :, None] - b[..., None, :] + ig[..., None, :], -jnp.inf)
    m_state = b + m[..., None]
    m_t = jnp.maximum(m_state, jnp.max(dmat, axis=-1))
    s_state = jnp.exp(m_state - m_t)
    w = jnp.exp(dmat - m_t[..., None]) * jnp.einsum('bhtd,bhsd->bhts', q, k)
    num = s_state[..., None] * jnp.einsum('bhtd,bhde->bhte', q, C) + jnp.einsum('bhts,bhse->bhte', w, v)
    den = s_state * jnp.einsum('bhtd,bhd->bht', q, n) + jnp.sum(w, axis=-1)
    h = num / jnp.maximum(jnp.abs(den), jnp.exp(-m_t))[..., None]
    g_end = b[..., -1]
    decay = g_end[..., None] - b + ig
    m_new = jnp.maximum(g_end + m, jnp.max(decay, axis=-1))
    s_old = jnp.exp(g_end + m - m_new)
    w_end = jnp.exp(decay - m_new[..., None])
    C_new = s_old[..., None, None] * C + jnp.einsum('bhs,bhsd,bhse->bhde', w_end, k, v)
    n_new = s_old[..., None] * n + jnp.einsum('bhs,bhsd->bhd', w_end, k)
    return (C_new, n_new, m_new), h


def _mlstm_mixer(xn, w_in, conv_w, conv_b, w_q, w_k, w_v, w_gate, b_gate, head_norm, skip, w_out):
    Bsz, T, _ = xn.shape
    proj = xn @ w_in
    x_m, o_pre, z = jnp.split(proj, 3, axis=-1)
    x_c = jax.nn.silu(_causal_conv(x_m, conv_w, conv_b))
    xc_h = x_c.reshape(Bsz, T, A_HEADS, A_HEAD_IN)
    xm_h = x_m.reshape(Bsz, T, A_HEADS, A_HEAD_IN)
    q = jnp.einsum('bthe,hed->bhtd', xc_h, w_q)
    k = jnp.einsum('bthe,hed->bhtd', xc_h, w_k)
    v = jnp.einsum('bthe,hed->bhtd', xm_h, w_v)
    nqk = A_HEADS * A_DQK
    gw_q = w_gate[:nqk].reshape(A_HEADS, A_DQK, 2 * A_HEADS)
    gw_k = w_gate[nqk:2 * nqk].reshape(A_HEADS, A_DQK, 2 * A_HEADS)
    gw_v = w_gate[2 * nqk:].reshape(A_HEADS, A_DV, 2 * A_HEADS)
    gates = (jnp.einsum('bhtd,hdg->btg', q, gw_q) + jnp.einsum('bhtd,hdg->btg', k, gw_k)
             + jnp.einsum('bhtd,hdg->btg', v, gw_v) + b_gate).astype(jnp.float32)
    ig = jnp.transpose(gates[..., :A_HEADS], (0, 2, 1))
    lf = jnp.transpose(jax.nn.log_sigmoid(gates[..., A_HEADS:]), (0, 2, 1))
    qf = q.astype(jnp.float32) * (A_DQK ** -0.5)
    kf = k.astype(jnp.float32)
    vf = v.astype(jnp.float32)
    carry0 = (jnp.zeros((Bsz, A_HEADS, A_DQK, A_DV), jnp.float32),
              jnp.zeros((Bsz, A_HEADS, A_DQK), jnp.float32),
              jnp.zeros((Bsz, A_HEADS), jnp.float32))
    carry, h_meta = _mlstm_chunk(carry0, (qf[:, :, :N_META], kf[:, :, :N_META], vf[:, :, :N_META],
                                          ig[:, :, :N_META], lf[:, :, :N_META]))
    nc = (T - N_META) // A_CHUNK

    def to_chunks(a):
        r = a[:, :, N_META:]
        r = r.reshape(a.shape[0], a.shape[1], nc, A_CHUNK, *a.shape[3:])
        return jnp.moveaxis(r, 2, 0)

    xs = (to_chunks(qf), to_chunks(kf), to_chunks(vf), to_chunks(ig), to_chunks(lf))
    _, h_real = lax.scan(_mlstm_chunk, carry, xs)
    h_real = jnp.moveaxis(h_real, 0, 2).reshape(Bsz, A_HEADS, T - N_META, A_DV)
    h = jnp.concatenate([h_meta, h_real], axis=2)
    h = jnp.transpose(h, (0, 2, 1, 3)) * jax.nn.sigmoid(o_pre.astype(jnp.float32)).reshape(Bsz, T, A_HEADS, A_DV)
    h = h * lax.rsqrt(jnp.mean(h * h, axis=-1, keepdims=True) + EPS)
    h = h.reshape(Bsz, T, A_INNER) * head_norm.astype(jnp.float32)
    y = (h + skip.astype(jnp.float32) * x_c.astype(jnp.float32)) * jax.nn.silu(z.astype(jnp.float32))
    return y.astype(xn.dtype) @ w_out


def _fox_block(qb, cq, qpos, k, v, ck, kpos):
    s = jnp.einsum('bhqd,bhkd->bhqk', qb, k).astype(jnp.float32) * (B_HEAD_DIM ** -0.5)
    s = s + (cq[..., :, None] - ck[..., None, :])
    s = jnp.where(kpos[None, :] <= qpos[:, None], s, -jnp.inf)
    p = jax.nn.softmax(s, axis=-1)
    return jnp.einsum('bhqk,bhkd->bhqd', p.astype(v.dtype), v)


def _fox_mixer(xn, w_in, b_f, q_norm, k_norm, w_out):
    Bsz, T, _ = xn.shape
    proj = xn @ w_in
    q, k, v, f_pre, z = jnp.split(proj, [B_INNER, 2 * B_INNER, 3 * B_INNER, 3 * B_INNER + B_HEADS], axis=-1)

    def heads(a):
        return jnp.transpose(a.reshape(Bsz, T, B_HEADS, B_HEAD_DIM), (0, 2, 1, 3))

    q = _rmsnorm(heads(q), q_norm)
    k = _rmsnorm(heads(k), k_norm)
    v = heads(v)
    logf = jnp.transpose(jax.nn.log_sigmoid((f_pre + b_f).astype(jnp.float32)), (0, 2, 1))
    c = jnp.cumsum(logf, axis=-1)
    kpos = jnp.arange(T)
    meta_pos = jnp.arange(N_META)
    o_meta = _fox_block(q[:, :, :N_META], c[:, :, :N_META], meta_pos,
                        k[:, :, :N_META], v[:, :, :N_META], c[:, :, :N_META], meta_pos)
    nb = (T - N_META) // B_QBLOCK
    q_r = jnp.moveaxis(q[:, :, N_META:].reshape(Bsz, B_HEADS, nb, B_QBLOCK, B_HEAD_DIM), 2, 0)
    c_r = jnp.moveaxis(c[:, :, N_META:].reshape(Bsz, B_HEADS, nb, B_QBLOCK), 2, 0)
    pos_r = (N_META + jnp.arange(T - N_META)).reshape(nb, B_QBLOCK)
    o_r = lax.map(lambda blk: _fox_block(blk[0], blk[1], blk[2], k, v, c, kpos), (q_r, c_r, pos_r))
    o_r = jnp.moveaxis(o_r, 0, 2).reshape(Bsz, B_HEADS, T - N_META, B_HEAD_DIM)
    o = jnp.concatenate([o_meta, o_r], axis=2)
    o = jnp.transpose(o, (0, 2, 1, 3)).reshape(Bsz, T, B_INNER)
    y = o.astype(jnp.float32) * jax.nn.silu(z.astype(jnp.float32))
    return y.astype(xn.dtype) @ w_out


def setup_inputs(seed: int = 0) -> dict:
    key = jax.random.key(seed)
    ks = jax.random.split(key, 24)
    nrm = jax.random.normal
    f32 = jnp.float32
    x = nrm(ks[0], (BATCH, SEQ, D_MODEL), f32)
    meta_tokens = nrm(ks[1], (N_META, D_MODEL), f32)
    a_norm = 1.0 + 0.02 * nrm(ks[2], (N_A, D_MODEL), f32)
    a_w_in = nrm(ks[3], (N_A, D_MODEL, 3 * A_INNER), f32) * D_MODEL ** -0.5
    a_conv_w = nrm(ks[4], (N_A, A_CONV, A_INNER), f32) * A_CONV ** -0.5
    a_conv_b = 0.02 * nrm(ks[5], (N_A, A_INNER), f32)
    a_w_q = nrm(ks[6], (N_A, A_HEADS, A_HEAD_IN, A_DQK), f32) * A_HEAD_IN ** -0.5
    a_w_k = nrm(ks[7], (N_A, A_HEADS, A_HEAD_IN, A_DQK), f32) * A_HEAD_IN ** -0.5
    a_w_v = nrm(ks[8], (N_A, A_HEADS, A_HEAD_IN, A_DV), f32) * A_HEAD_IN ** -0.5
    gate_in = 2 * A_HEADS * A_DQK + A_HEADS * A_DV
    a_w_gate = nrm(ks[9], (N_A, gate_in, 2 * A_HEADS), f32) * gate_in ** -0.5
    b_in = 0.1 * nrm(ks[10], (N_A, A_HEADS), f32)
    b_fg = jnp.linspace(3.0, 6.0, A_HEADS, dtype=f32)[None, :] + 0.1 * nrm(ks[11], (N_A, A_HEADS), f32)
    a_b_gate = jnp.concatenate([b_in, b_fg], axis=-1)
    a_head_norm = 1.0 + 0.02 * nrm(ks[12], (N_A, A_INNER), f32)
    a_skip = 1.0 + 0.02 * nrm(ks[13], (N_A, A_INNER), f32)
    a_w_out = nrm(ks[14], (N_A, A_INNER, D_MODEL), f32) * A_INNER ** -0.5
    b_norm = 1.0 + 0.02 * nrm(ks[15], (N_B, D_MODEL), f32)
    b_w_in = nrm(ks[16], (N_B, D_MODEL, 4 * B_INNER + B_HEADS), f32) * D_MODEL ** -0.5
    b_b_f = 3.0 + 0.5 * nrm(ks[17], (N_B, B_HEADS), f32)
    b_q_norm = 1.0 + 0.02 * nrm(ks[18], (N_B, B_HEAD_DIM), f32)
    b_k_norm = 1.0 + 0.02 * nrm(ks[19], (N_B, B_HEAD_DIM), f32)
    b_w_out = nrm(ks[20], (N_B, B_INNER, D_MODEL), f32) * B_INNER ** -0.5
    final_norm = 1.0 + 0.02 * nrm(ks[21], (D_MODEL,), f32)
    return {'x': x, 'meta_tokens': meta_tokens,
            'a_norm': a_norm, 'a_w_in': a_w_in, 'a_conv_w': a_conv_w, 'a_conv_b': a_conv_b,
            'a_w_q': a_w_q, 'a_w_k': a_w_k, 'a_w_v': a_w_v, 'a_w_gate': a_w_gate, 'a_b_gate': a_b_gate,
            'a_head_norm': a_head_norm, 'a_skip': a_skip, 'a_w_out': a_w_out,
            'b_norm': b_norm, 'b_w_in': b_w_in, 'b_b_f': b_b_f, 'b_q_norm': b_q_norm,
            'b_k_norm': b_k_norm, 'b_w_out': b_w_out, 'final_norm': final_norm}


def reference(x, meta_tokens, a_norm, a_w_in, a_conv_w, a_conv_b, a_w_q, a_w_k, a_w_v, a_w_gate,
              a_b_gate, a_head_norm, a_skip, a_w_out, b_norm, b_w_in, b_b_f, b_q_norm, b_k_norm,
              b_w_out, final_norm):
    Bsz = x.shape[0]
    meta = jnp.broadcast_to(meta_tokens[None].astype(x.dtype), (Bsz, N_META, D_MODEL))
    h = jnp.concatenate([meta, x], axis=1)
    for i in range(DEPTH):
        j = i // N_MIXERS
        if i % N_MIXERS == 0:
            h = h + _mlstm_mixer(_rmsnorm(h, a_norm[j]), a_w_in[j], a_conv_w[j], a_conv_b[j],
                                 a_w_q[j], a_w_k[j], a_w_v[j], a_w_gate[j], a_b_gate[j],
                                 a_head_norm[j], a_skip[j], a_w_out[j])
        else:
            h = h + _fox_mixer(_rmsnorm(h, b_norm[j]), b_w_in[j], b_b_f[j], b_q_norm[j],
                               b_k_norm[j], b_w_out[j])
    return _rmsnorm(h, final_norm)[:, N_META:]
```

```python
import functools

import jax
import jax.numpy as jnp
from jax import lax
from jax.experimental import pallas as pl
from jax.experimental.pallas import tpu as pltpu

F32 = jnp.float32
BF16 = jnp.bfloat16
EPS = 1e-6
NEG_INF = float("-inf")

V7X_VMEM_BYTES = 64 * 1024 * 1024
V7X_LANES = 128
VMEM_LIMIT = 56 * 1024 * 1024

A_HEADS = 8
A_CONV = 4
B_HEAD_DIM = 128
MLSTM_CHUNK = 256
FOX_QBLOCK = 256
CONV_ROWS = 64


def _cparams(*sem):
    return pltpu.CompilerParams(dimension_semantics=sem, vmem_limit_bytes=VMEM_LIMIT)


def _sigmoid(x):
    return 1.0 / (1.0 + jnp.exp(-x))


def _log_sigmoid(x):
    return jnp.minimum(x, 0.0) - jnp.log(1.0 + jnp.exp(-jnp.abs(x)))


def _row_to_col(row, eye):
    return jnp.sum(jnp.where(eye, row, 0.0), axis=1, keepdims=True)


def _col_to_row(col, eye):
    return jnp.sum(jnp.where(eye, col, 0.0), axis=0, keepdims=True)


def _cumsum_row(row, ti, si):
    col = jnp.sum(jnp.where(si <= ti, row, 0.0), axis=1, keepdims=True)
    return col, _col_to_row(col, ti == si)


def _norm_matmul_kernel(h_ref, g_ref, w_ref, o_ref, xn_ref):
    @pl.when(pl.program_id(1) == 0)
    def _():
        x = h_ref[...]
        ms = jnp.mean(x * x, axis=-1, keepdims=True)
        xn_ref[...] = (x * lax.rsqrt(ms + EPS) * g_ref[...]).astype(xn_ref.dtype)

    o_ref[...] = jnp.dot(xn_ref[...], w_ref[...], preferred_element_type=F32).astype(o_ref.dtype)


def _norm_matmul_extra_kernel(h_ref, g_ref, w_ref, wx_ref, o_ref, ox_ref, xn_ref):
    @pl.when(pl.program_id(1) == 0)
    def _():
        x = h_ref[...]
        ms = jnp.mean(x * x, axis=-1, keepdims=True)
        xn = (x * lax.rsqrt(ms + EPS) * g_ref[...]).astype(xn_ref.dtype)
        xn_ref[...] = xn
        ox_ref[...] = jnp.dot(xn, wx_ref[...], preferred_element_type=F32)

    o_ref[...] = jnp.dot(xn_ref[...], w_ref[...], preferred_element_type=F32).astype(o_ref.dtype)


def _norm_matmul(h, g, w, wx=None, *, tm, tn, name):
    m, d = h.shape
    n = w.shape[1]
    assert m % tm == 0 and n % tn == 0
    grid = (m // tm, n // tn)
    h_spec = pl.BlockSpec((tm, d), lambda i, j: (i, 0))
    g_spec = pl.BlockSpec((1, d), lambda i, j: (0, 0))
    w_spec = pl.BlockSpec((d, tn), lambda i, j: (0, j))
    o_spec = pl.BlockSpec((tm, tn), lambda i, j: (i, j))
    scratch = [pltpu.VMEM((tm, d), BF16)]
    if wx is None:
        return pl.pallas_call(
            _norm_matmul_kernel,
            grid=grid,
            in_specs=[h_spec, g_spec, w_spec],
            out_specs=o_spec,
            out_shape=jax.ShapeDtypeStruct((m, n), BF16),
            scratch_shapes=scratch,
            compiler_params=_cparams("parallel", "arbitrary"),
            name=name,
        )(h, g, w)
    nx = wx.shape[1]
    return pl.pallas_call(
        _norm_matmul_extra_kernel,
        grid=grid,
        in_specs=[h_spec, g_spec, w_spec, pl.BlockSpec((d, nx), lambda i, j: (0, 0))],
        out_specs=[o_spec, pl.BlockSpec((tm, nx), lambda i, j: (i, 0))],
        out_shape=[jax.ShapeDtypeStruct((m, n), BF16), jax.ShapeDtypeStruct((m, nx), F32)],
        scratch_shapes=scratch,
        compiler_params=_cparams("parallel", "arbitrary"),
        name=name,
    )(h, g, w, wx)


def _matmul_residual_kernel(y_ref, w_ref, r_ref, o_ref):
    o_ref[...] = r_ref[...] + jnp.dot(y_ref[...], w_ref[...], preferred_element_type=F32)


def _matmul_residual(y, w, res, *, tm, tn, name):
    m, k = y.shape
    n = w.shape[1]
    assert m % tm == 0 and n % tn == 0
    return pl.pallas_call(
        _matmul_residual_kernel,
        grid=(m // tm, n // tn),
        in_specs=[
            pl.BlockSpec((tm, k), lambda i, j: (i, 0)),
            pl.BlockSpec((k, tn), lambda i, j: (0, j)),
            pl.BlockSpec((tm, tn), lambda i, j: (i, j)),
        ],
        out_specs=pl.BlockSpec((tm, tn), lambda i, j: (i, j)),
        out_shape=jax.ShapeDtypeStruct((m, n), F32),
        compiler_params=_cparams("parallel", "parallel"),
        name=name,
    )(y, w, res)


def _matmul_residual_norm_kernel(y_ref, w_ref, r_ref, g_ref, o_ref):
    x = r_ref[0] + jnp.dot(y_ref[0], w_ref[...], preferred_element_type=F32)
    ms = jnp.mean(x * x, axis=-1, keepdims=True)
    o_ref[0] = x * lax.rsqrt(ms + EPS) * g_ref[...]


def _matmul_residual_norm(y, w, res, g, *, seq, tm, name):
    bsz, _, k = y.shape
    n = w.shape[1]
    assert seq % tm == 0
    return pl.pallas_call(
        _matmul_residual_norm_kernel,
        grid=(bsz, seq // tm),
        in_specs=[
            pl.BlockSpec((1, tm, k), lambda b, i: (b, i, 0)),
            pl.BlockSpec((k, n), lambda b, i: (0, 0)),
            pl.BlockSpec((1, tm, n), lambda b, i: (b, i, 0)),
            pl.BlockSpec((1, n), lambda b, i: (0, 0)),
        ],
        out_specs=pl.BlockSpec((1, tm, n), lambda b, i: (b, i, 0)),
        out_shape=jax.ShapeDtypeStruct((bsz, seq, n), F32),
        compiler_params=_cparams("parallel", "parallel"),
        name=name,
    )(y, w, res, g)


def _conv_qkv_kernel(xm_ref, cw_ref, cb_ref, wq_ref, wk_ref, wv_ref, gq_ref, gk_ref, gv_ref, bg_ref,
                     xc_ref, q_ref, k_ref, v_ref, g_ref, pad_ref, *, seq, n_meta):
    total = seq + n_meta
    rows = CONV_ROWS
    lead = 8

    def fill(i, carry):
        r0 = pl.multiple_of(i * rows, rows)
        pad_ref[pl.ds(lead + r0, rows), :] = xm_ref[0, pl.ds(r0, rows), :].astype(F32)
        return carry

    lax.fori_loop(0, seq // rows, fill, 0)
    pad_ref[lead + seq:lead + total, :] = xm_ref[0, seq:total, :].astype(F32)
    pad_ref[0:lead, :] = pad_ref[total:lead + total, :]

    cw = cw_ref[...]
    cb = cb_ref[...]

    def taps(window, n_rows):
        acc = window[lead:lead + n_rows] * cw[A_CONV - 1:A_CONV] + cb
        for s in range(1, A_CONV):
            acc = acc + window[lead - s:lead - s + n_rows] * cw[A_CONV - 1 - s:A_CONV - s]
        return acc

    def conv(i, carry):
        r0 = pl.multiple_of(i * rows, rows)
        acc = taps(pad_ref[pl.ds(r0, rows + lead), :], rows)
        xc_ref[0, pl.ds(r0, rows), :] = (acc * _sigmoid(acc)).astype(xc_ref.dtype)
        return carry

    lax.fori_loop(0, seq // rows, conv, 0)

    window = pad_ref[seq:seq + lead + n_meta, :]
    row = lax.broadcasted_iota(jnp.int32, (n_meta, 1), 0)
    acc = window[lead:lead + n_meta] * cw[A_CONV - 1:A_CONV] + cb
    for s in range(1, A_CONV):
        shifted = window[lead - s:lead - s + n_meta]
        acc = acc + jnp.where(row >= s, shifted, 0.0) * cw[A_CONV - 1 - s:A_CONV - s]
    xc_ref[0, seq:total, :] = (acc * _sigmoid(acc)).astype(xc_ref.dtype)

    xc = xc_ref[0]
    q = jnp.dot(xc, wq_ref[0], preferred_element_type=F32).astype(BF16)
    k = jnp.dot(xc, wk_ref[0], preferred_element_type=F32).astype(BF16)
    v = jnp.dot(xm_ref[0], wv_ref[0], preferred_element_type=F32).astype(BF16)
    q_ref[0] = q
    k_ref[0] = k
    v_ref[0] = v

    @pl.when(pl.program_id(1) == 0)
    def _():
        g_ref[0] = jnp.broadcast_to(bg_ref[...], g_ref.shape[1:])

    g_ref[0] += (jnp.dot(q, gq_ref[0], preferred_element_type=F32)
                 + jnp.dot(k, gk_ref[0], preferred_element_type=F32)
                 + jnp.dot(v, gv_ref[0], preferred_element_type=F32))


def _conv_qkv(proj, cw, cb, wq, wk, wv, gq, gk, gv, bg, *, seq, n_meta):
    bsz, total, _ = proj.shape
    heads, hin, dqk = wq.shape
    dv = wv.shape[2]
    ng = gq.shape[2]
    e = heads * hin
    kern = functools.partial(_conv_qkv_kernel, seq=seq, n_meta=n_meta)
    return pl.pallas_call(
        kern,
        grid=(bsz, heads),
        in_specs=[
            pl.BlockSpec((1, total, hin), lambda b, h: (b, 0, h)),
            pl.BlockSpec((A_CONV, hin), lambda b, h: (0, h)),
            pl.BlockSpec((1, hin), lambda b, h: (0, h)),
            pl.BlockSpec((1, hin, dqk), lambda b, h: (h, 0, 0)),
            pl.BlockSpec((1, hin, dqk), lambda b, h: (h, 0, 0)),
            pl.BlockSpec((1, hin, dv), lambda b, h: (h, 0, 0)),
            pl.BlockSpec((1, dqk, ng), lambda b, h: (h, 0, 0)),
            pl.BlockSpec((1, dqk, ng), lambda b, h: (h, 0, 0)),
            pl.BlockSpec((1, dv, ng), lambda b, h: (h, 0, 0)),
            pl.BlockSpec((1, ng), lambda b, h: (0, 0)),
        ],
        out_specs=[
            pl.BlockSpec((1, total, hin), lambda b, h: (b, 0, h)),
            pl.BlockSpec((1, total, dqk), lambda b, h: (b, 0, h)),
            pl.BlockSpec((1, total, dqk), lambda b, h: (b, 0, h)),
            pl.BlockSpec((1, total, dv), lambda b, h: (b, 0, h)),
            pl.BlockSpec((1, total, ng), lambda b, h: (b, 0, 0)),
        ],
        out_shape=[
            jax.ShapeDtypeStruct((bsz, total, e), BF16),
            jax.ShapeDtypeStruct((bsz, total, heads * dqk), BF16),
            jax.ShapeDtypeStruct((bsz, total, heads * dqk), BF16),
            jax.ShapeDtypeStruct((bsz, total, heads * dv), BF16),
            jax.ShapeDtypeStruct((bsz, total, ng), F32),
        ],
        scratch_shapes=[pltpu.VMEM((total + 8, hin), F32)],
        compiler_params=_cparams("parallel", "arbitrary"),
        name="mlstm_conv_qkv",
    )(proj, cw, cb, wq, wk, wv, gq, gk, gv, bg)


def _mlstm_chunk(q, k, v, ig, f_pre, c_ref, n_ref, m_ref, scale):
    length = q.shape[0]
    ti = lax.broadcasted_iota(jnp.int32, (length, length), 0)
    si = lax.broadcasted_iota(jnp.int32, (length, length), 1)
    eye = ti == si
    b_col, b = _cumsum_row(_log_sigmoid(f_pre), ti, si)
    m_prev = m_ref[...]
    dmat = jnp.where(si <= ti, b_col - b + ig, NEG_INF)
    m_state = b_col + m_prev
    m_t = jnp.maximum(m_state, jnp.max(dmat, axis=1, keepdims=True))
    s_state = jnp.exp(m_state - m_t)
    qs = (q.astype(F32) * scale).astype(BF16)
    qk = lax.dot_general(qs, k, (((1,), (1,)), ((), ())), preferred_element_type=F32)
    w = jnp.exp(dmat - m_t) * qk
    c_old = c_ref[...]
    n_old = n_ref[...]
    num = s_state * jnp.dot(qs, c_old.astype(BF16), preferred_element_type=F32)
    num = num + jnp.dot(w.astype(BF16), v, preferred_element_type=F32)
    qn = jnp.sum(qs.astype(F32) * n_old, axis=1, keepdims=True)
    den = s_state * qn + jnp.sum(w, axis=1, keepdims=True)
    h = num / jnp.maximum(jnp.abs(den), jnp.exp(-m_t))

    g_end = b[:, length - 1:length]
    decay = g_end - b + ig
    m_new = jnp.maximum(g_end + m_prev, jnp.max(decay, axis=1, keepdims=True))
    s_old = jnp.exp(g_end + m_prev - m_new)
    w_end = _row_to_col(jnp.exp(decay - m_new), eye)
    kw = k.astype(F32) * w_end
    c_ref[...] = s_old * c_old + lax.dot_general(
        kw.astype(BF16), v, (((0,), (0,)), ((), ())), preferred_element_type=F32)
    n_ref[...] = s_old * n_old + jnp.sum(kw, axis=0, keepdims=True)
    m_ref[...] = m_new
    return h


def _mlstm_kernel(q_ref, k_ref, v_ref, o_ref, z_ref, xc_ref, hn_ref, sk_ref,
                  igr_ref, fr_ref, igm_ref, fm_ref, y_ref,
                  c_ref, n_ref, m_ref, *, seq, n_meta, chunk, scale):
    c_ref[...] = jnp.zeros_like(c_ref)
    n_ref[...] = jnp.zeros_like(n_ref)
    m_ref[...] = jnp.zeros_like(m_ref)
    hn = hn_ref[...]
    sk = sk_ref[...]

    def emit(h, r0, n_rows):
        rows = pl.ds(r0, n_rows)
        hg = h * _sigmoid(o_ref[0, rows, :].astype(F32))
        ms = jnp.mean(hg * hg, axis=-1, keepdims=True)
        hg = hg * lax.rsqrt(ms + EPS) * hn
        z = z_ref[0, rows, :].astype(F32)
        y = (hg + sk * xc_ref[0, rows, :].astype(F32)) * (z * _sigmoid(z))
        y_ref[0, rows, :] = y.astype(y_ref.dtype)

    rows = pl.ds(seq, n_meta)
    h = _mlstm_chunk(q_ref[0, rows, :], k_ref[0, rows, :], v_ref[0, rows, :],
                     igm_ref[0, 0], fm_ref[0, 0], c_ref, n_ref, m_ref, scale)
    emit(h, seq, n_meta)

    def body(c, carry):
        r0 = pl.multiple_of(c * chunk, chunk)
        rows = pl.ds(r0, chunk)
        h = _mlstm_chunk(q_ref[0, rows, :], k_ref[0, rows, :], v_ref[0, rows, :],
                         igr_ref[0, 0, pl.ds(c, 1), :], fr_ref[0, 0, pl.ds(c, 1), :],
                         c_ref, n_ref, m_ref, scale)
        emit(h, r0, chunk)
        return carry

    lax.fori_loop(0, seq // chunk, body, 0)


def _mlstm(q, k, v, proj, xc, hn, sk, ig_real, f_real, ig_meta, f_meta, *, seq, n_meta):
    bsz, total, _ = q.shape
    heads = A_HEADS
    dqk = q.shape[2] // heads
    dv = v.shape[2] // heads
    e = heads * dv
    chunk = MLSTM_CHUNK
    nc = seq // chunk
    kern = functools.partial(_mlstm_kernel, seq=seq, n_meta=n_meta, chunk=chunk, scale=dqk ** -0.5)
    o_blk = e // dv
    return pl.pallas_call(
        kern,
        grid=(bsz, heads),
        in_specs=[
            pl.BlockSpec((1, total, dqk), lambda b, h: (b, 0, h)),
            pl.BlockSpec((1, total, dqk), lambda b, h: (b, 0, h)),
            pl.BlockSpec((1, total, dv), lambda b, h: (b, 0, h)),
            pl.BlockSpec((1, total, dv), lambda b, h: (b, 0, o_blk + h)),
            pl.BlockSpec((1, total, dv), lambda b, h: (b, 0, 2 * o_blk + h)),
            pl.BlockSpec((1, total, dv), lambda b, h: (b, 0, h)),
            pl.BlockSpec((1, dv), lambda b, h: (0, h)),
            pl.BlockSpec((1, dv), lambda b, h: (0, h)),
            pl.BlockSpec((1, 1, nc, chunk), lambda b, h: (b, h, 0, 0)),
            pl.BlockSpec((1, 1, nc, chunk), lambda b, h: (b, h, 0, 0)),
            pl.BlockSpec((1, 1, 1, n_meta), lambda b, h: (b, h, 0, 0)),
            pl.BlockSpec((1, 1, 1, n_meta), lambda b, h: (b, h, 0, 0)),
        ],
        out_specs=pl.BlockSpec((1, total, dv), lambda b, h: (b, 0, h)),
        out_shape=jax.ShapeDtypeStruct((bsz, total, e), BF16),
        scratch_shapes=[
            pltpu.VMEM((dqk, dv), F32),
            pltpu.VMEM((1, dqk), F32),
            pltpu.VMEM((1, 1), F32),
        ],
        compiler_params=_cparams("parallel", "parallel"),
        name="mlstm_recurrence",
    )(q, k, v, proj, proj, xc, hn, sk, ig_real, f_real, ig_meta, f_meta)


def _fox_kernel(q_ref, k_ref, v_ref, z_ref, qn_ref, kn_ref, bf_ref, fr_ref, fm_ref, y_ref,
                kn_scr, c_scr, *, seq, n_meta, qblock, scale):
    total = seq + n_meta
    nb = seq // qblock
    qg = qn_ref[...]
    kg = kn_ref[...]

    def norm(x, g):
        x = x.astype(F32)
        ms = jnp.mean(x * x, axis=-1, keepdims=True)
        return (x * lax.rsqrt(ms + EPS) * g).astype(BF16)

    def norm_k(i, carry):
        r0 = pl.multiple_of(i * qblock, qblock)
        kn_scr[pl.ds(r0, qblock), :] = norm(k_ref[0, pl.ds(r0, qblock), :], kg)
        return carry

    lax.fori_loop(0, nb, norm_k, 0)
    kn_scr[seq:total, :] = norm(k_ref[0, seq:total, :], kg)

    bias = bf_ref[0]

    def finish(acc, l, r0, n_rows):
        rows = pl.ds(r0, n_rows)
        z = z_ref[0, rows, :].astype(F32)
        y_ref[0, rows, :] = ((acc / l) * (z * _sigmoid(z))).astype(y_ref.dtype)

    ti = lax.broadcasted_iota(jnp.int32, (n_meta, n_meta), 0)
    si = lax.broadcasted_iota(jnp.int32, (n_meta, n_meta), 1)
    c_meta_col, c_meta = _cumsum_row(_log_sigmoid(fm_ref[0, 0] + bias), ti, si)
    qm = norm(q_ref[0, seq:total, :], qg)
    km = kn_scr[seq:total, :]
    vm = v_ref[0, seq:total, :]
    s = lax.dot_general(qm, km, (((1,), (1,)), ((), ())), preferred_element_type=F32) * scale
    s = s + (c_meta_col - c_meta)
    s = jnp.where(si <= ti, s, NEG_INF)
    p = jnp.exp(s - jnp.max(s, axis=1, keepdims=True))
    finish(jnp.dot(p.astype(BF16), vm, preferred_element_type=F32),
           jnp.sum(p, axis=1, keepdims=True), seq, n_meta)

    ti = lax.broadcasted_iota(jnp.int32, (qblock, qblock), 0)
    si = lax.broadcasted_iota(jnp.int32, (qblock, qblock), 1)
    eye = ti == si
    causal = si <= ti

    running = c_meta[:, n_meta - 1:n_meta]
    for j in range(nb):
        _, c_row = _cumsum_row(_log_sigmoid(fr_ref[0, 0, j:j + 1, :] + bias), ti, si)
        c_scr[j:j + 1, :] = c_row + running
        running = running + c_row[:, qblock - 1:qblock]

    for j in range(nb):
        r0 = j * qblock
        qb = norm(q_ref[0, r0:r0 + qblock, :], qg)
        cq = _row_to_col(c_scr[j:j + 1, :], eye)
        s_m = lax.dot_general(qb, km, (((1,), (1,)), ((), ())), preferred_element_type=F32) * scale
        s_m = s_m + (cq - c_meta)
        m_run = jnp.max(s_m, axis=1, keepdims=True)
        parts = []
        for i in range(j + 1):
            c0 = i * qblock
            s = lax.dot_general(qb, kn_scr[c0:c0 + qblock, :], (((1,), (1,)), ((), ())),
                                preferred_element_type=F32) * scale
            s = s + (cq - c_scr[i:i + 1, :])
            if i == j:
                s = jnp.where(causal, s, NEG_INF)
            parts.append(s)
            m_run = jnp.maximum(m_run, jnp.max(s, axis=1, keepdims=True))
        p_m = jnp.exp(s_m - m_run)
        l = jnp.sum(p_m, axis=1, keepdims=True)
        acc = jnp.dot(p_m.astype(BF16), vm, preferred_element_type=F32)
        for i in range(j + 1):
            c0 = i * qblock
            p = jnp.exp(parts[i] - m_run)
            l = l + jnp.sum(p, axis=1, keepdims=True)
            acc = acc + jnp.dot(p.astype(BF16), v_ref[0, c0:c0 + qblock, :], preferred_element_type=F32)
        finish(acc, l, r0, qblock)


def _fox(proj, qn, kn, bf, f_real, f_meta, *, seq, n_meta):
    bsz, total, width = proj.shape
    hd = B_HEAD_DIM
    inner = width // 4
    heads = inner // hd
    qblock = FOX_QBLOCK
    nb = seq // qblock
    kern = functools.partial(_fox_kernel, seq=seq, n_meta=n_meta, qblock=qblock, scale=hd ** -0.5)
    return pl.pallas_call(
        kern,
        grid=(bsz, heads),
        in_specs=[
            pl.BlockSpec((1, total, hd), lambda b, h: (b, 0, h)),
            pl.BlockSpec((1, total, hd), lambda b, h: (b, 0, heads + h)),
            pl.BlockSpec((1, total, hd), lambda b, h: (b, 0, 2 * heads + h)),
            pl.BlockSpec((1, total, hd), lambda b, h: (b, 0, 3 * heads + h)),
            pl.BlockSpec((1, hd), lambda b, h: (0, 0)),
            pl.BlockSpec((1, hd), lambda b, h: (0, 0)),
            pl.BlockSpec((1, 1, 1), lambda b, h: (h, 0, 0)),
            pl.BlockSpec((1, 1, nb, qblock), lambda b, h: (b, h, 0, 0)),
            pl.BlockSpec((1, 1, 1, n_meta), lambda b, h: (b, h, 0, 0)),
        ],
        out_specs=pl.BlockSpec((1, total, hd), lambda b, h: (b, 0, h)),
        out_shape=jax.ShapeDtypeStruct((bsz, total, inner), BF16),
        scratch_shapes=[
            pltpu.VMEM((total, hd), BF16),
            pltpu.VMEM((nb, qblock), F32),
        ],
        compiler_params=_cparams("parallel", "parallel"),
        name="fox_attention",
    )(proj, proj, proj, proj, qn, kn, bf, f_real, f_meta)


def _gate_rows(g, seq, n_meta, block):
    bsz, _, heads = g.shape
    gt = jnp.transpose(g, (0, 2, 1))
    real = gt[:, :, :seq].reshape(bsz, heads, seq // block, block)
    meta = gt[:, :, seq:].reshape(bsz, heads, 1, n_meta)
    return real, meta


def kernel(x, meta_tokens, a_norm, a_w_in, a_conv_w, a_conv_b, a_w_q, a_w_k, a_w_v, a_w_gate, a_b_gate,
           a_head_norm, a_skip, a_w_out, b_norm, b_w_in, b_b_f, b_q_norm, b_k_norm, b_w_out, final_norm):
    bsz, seq, d = x.shape
    n_meta = meta_tokens.shape[0]
    total = seq + n_meta
    m = bsz * total
    depth_a = a_norm.shape[0]
    depth_b = b_norm.shape[0]
    assert depth_a == 1 and depth_b == 1, "layer schedule: one mLSTM layer followed by one FoX layer"
    tm = total // 3
    assert total % 3 == 0 and tm % 16 == 0

    meta = jnp.broadcast_to(meta_tokens[None].astype(x.dtype), (bsz, n_meta, d))
    h0 = jnp.concatenate([x, meta], axis=1).reshape(m, d)

    heads = A_HEADS
    e = a_w_out.shape[1]
    dqk = a_w_q.shape[3]
    dv = a_w_v.shape[3]
    nqk = heads * dqk
    w_gate = a_w_gate[0].astype(BF16)
    gq = w_gate[:nqk].reshape(heads, dqk, 2 * heads)
    gk = w_gate[nqk:2 * nqk].reshape(heads, dqk, 2 * heads)
    gv = w_gate[2 * nqk:].reshape(heads, dv, 2 * heads)

    proj = _norm_matmul(h0, a_norm[0][None], a_w_in[0].astype(BF16), tm=2 * tm, tn=1024, name="mlstm_in_proj")
    proj = proj.reshape(bsz, total, 3 * e)
    xc, q, k, v, gates = _conv_qkv(
        proj, a_conv_w[0], a_conv_b[0][None], a_w_q[0].astype(BF16), a_w_k[0].astype(BF16),
        a_w_v[0].astype(BF16), gq, gk, gv, a_b_gate[0][None], seq=seq, n_meta=n_meta)
    ig_real, ig_meta = _gate_rows(gates[..., :heads], seq, n_meta, MLSTM_CHUNK)
    f_real, f_meta = _gate_rows(gates[..., heads:], seq, n_meta, MLSTM_CHUNK)
    y = _mlstm(q, k, v, proj, xc, a_head_norm[0][None], a_skip[0][None],
               ig_real, f_real, ig_meta, f_meta, seq=seq, n_meta=n_meta)
    h1 = _matmul_residual(y.reshape(m, e), a_w_out[0].astype(BF16), h0, tm=tm, tn=1024, name="mlstm_out_proj")

    inner = b_w_out.shape[1]
    bheads = inner // B_HEAD_DIM
    w_in = b_w_in[0]
    w_main = jnp.concatenate([w_in[:, :3 * inner], w_in[:, 3 * inner + bheads:]], axis=1).astype(BF16)
    w_f = jnp.pad(w_in[:, 3 * inner:3 * inner + bheads], ((0, 0), (0, V7X_LANES - bheads))).astype(BF16)
    projb, f_pre = _norm_matmul(h1, b_norm[0][None], w_main, w_f, tm=2 * tm, tn=1024, name="fox_in_proj")
    fr, fm = _gate_rows(f_pre[:, :bheads].reshape(bsz, total, bheads), seq, n_meta, FOX_QBLOCK)
    yb = _fox(projb.reshape(bsz, total, 4 * inner), b_q_norm[0][None], b_k_norm[0][None],
              b_b_f[0].reshape(bheads, 1, 1), fr, fm, seq=seq, n_meta=n_meta)
    return _matmul_residual_norm(yb, b_w_out[0].astype(BF16), h1.reshape(bsz, total, d), final_norm[None],
                                 seq=seq, tm=512, name="fox_out_proj_norm")
```

```python
import functools

import jax
import jax.numpy as jnp
from jax import lax
from jax.experimental import pallas as pl
from jax.experimental.pallas import tpu as pltpu

F32 = jnp.float32
BF16 = jnp.bfloat16
EPS = 1e-6
NEG_INF = float("-inf")

V7X_VMEM_BYTES = 64 * 1024 * 1024
V7X_LANES = 128
VMEM_LIMIT = 56 * 1024 * 1024

A_HEADS = 8
A_CONV = 4
B_HEAD_DIM = 128
MLSTM_CHUNK = 256
FOX_QBLOCK = 256
CONV_ROWS = 64


def _cparams(*sem):
    return pltpu.CompilerParams(dimension_semantics=sem, vmem_limit_bytes=VMEM_LIMIT)


def _sigmoid(x):
    return 1.0 / (1.0 + jnp.exp(-x))


def _log_sigmoid(x):
    return jnp.minimum(x, 0.0) - jnp.log(1.0 + jnp.exp(-jnp.abs(x)))


def _row_to_col(row, eye):
    return jnp.sum(jnp.where(eye, row, 0.0), axis=1, keepdims=True)


def _col_to_row(col, eye):
    return jnp.sum(jnp.where(eye, col, 0.0), axis=0, keepdims=True)


def _cumsum_row(row, ti, si):
    col = jnp.sum(jnp.where(si <= ti, row, 0.0), axis=1, keepdims=True)
    return col, _col_to_row(col, ti == si)


def _norm_matmul_kernel(h_ref, g_ref, w_ref, o_ref, xn_ref):
    @pl.when(pl.program_id(1) == 0)
    def _():
        x = h_ref[...]
        ms = jnp.mean(x * x, axis=-1, keepdims=True)
        xn_ref[...] = (x * lax.rsqrt(ms + EPS) * g_ref[...]).astype(xn_ref.dtype)

    o_ref[...] = jnp.dot(xn_ref[...], w_ref[...], preferred_element_type=F32).astype(o_ref.dtype)


def _norm_matmul_extra_kernel(h_ref, g_ref, w_ref, wx_ref, o_ref, ox_ref, xn_ref):
    @pl.when(pl.program_id(1) == 0)
    def _():
        x = h_ref[...]
        ms = jnp.mean(x * x, axis=-1, keepdims=True)
        xn = (x * lax.rsqrt(ms + EPS) * g_ref[...]).astype(xn_ref.dtype)
        xn_ref[...] = xn
        ox_ref[...] = jnp.dot(xn, wx_ref[...], preferred_element_type=F32)

    o_ref[...] = jnp.dot(xn_ref[...], w_ref[...], preferred_element_type=F32).astype(o_ref.dtype)


def _norm_matmul(h, g, w, wx=None, *, tm, tn, name):
    m, d = h.shape
    n = w.shape[1]
    assert m % tm == 0 and n % tn == 0
    grid = (m // tm, n // tn)
    h_spec = pl.BlockSpec((tm, d), lambda i, j: (i, 0))
    g_spec = pl.BlockSpec((1, d), lambda i, j: (0, 0))
    w_spec = pl.BlockSpec((d, tn), lambda i, j: (0, j))
    o_spec = pl.BlockSpec((tm, tn), lambda i, j: (i, j))
    scratch = [pltpu.VMEM((tm, d), BF16)]
    if wx is None:
        return pl.pallas_call(
            _norm_matmul_kernel,
            grid=grid,
            in_specs=[h_spec, g_spec, w_spec],
            out_specs=o_spec,
            out_shape=jax.ShapeDtypeStruct((m, n), BF16),
            scratch_shapes=scratch,
            compiler_params=_cparams("parallel", "arbitrary"),
            name=name,
        )(h, g, w)
    nx = wx.shape[1]
    return pl.pallas_call(
        _norm_matmul_extra_kernel,
        grid=grid,
        in_specs=[h_spec, g_spec, w_spec, pl.BlockSpec((d, nx), lambda i, j: (0, 0))],
        out_specs=[o_spec, pl.BlockSpec((tm, nx), lambda i, j: (i, 0))],
        out_shape=[jax.ShapeDtypeStruct((m, n), BF16), jax.ShapeDtypeStruct((m, nx), F32)],
        scratch_shapes=scratch,
        compiler_params=_cparams("parallel", "arbitrary"),
        name=name,
    )(h, g, w, wx)


def _matmul_residual_kernel(y_ref, w_ref, r_ref, o_ref):
    o_ref[...] = r_ref[...] + jnp.dot(y_ref[...], w_ref[...], preferred_element_type=F32)


def _matmul_residual(y, w, res, *, tm, tn, name):
    m, k = y.shape
    n = w.shape[1]
    assert m % tm == 0 and n % tn == 0
    return pl.pallas_call(
        _matmul_residual_kernel,
        grid=(m // tm, n // tn),
        in_specs=[
            pl.BlockSpec((tm, k), lambda i, j: (i, 0)),
            pl.BlockSpec((k, tn), lambda i, j: (0, j)),
            pl.BlockSpec((tm, tn), lambda i, j: (i, j)),
        ],
        out_specs=pl.BlockSpec((tm, tn), lambda i, j: (i, j)),
        out_shape=jax.ShapeDtypeStruct((m, n), F32),
        compiler_params=_cparams("parallel", "parallel"),
        name=name,
    )(y, w, res)


def _matmul_residual_norm_kernel(y_ref, w_ref, r_ref, g_ref, o_ref):
    x = r_ref[0] + jnp.dot(y_ref[0], w_ref[...], preferred_element_type=F32)
    ms = jnp.mean(x * x, axis=-1, keepdims=True)
    o_ref[0] = x * lax.rsqrt(ms + EPS) * g_ref[...]


def _matmul_residual_norm(y, w, res, g, *, seq, tm, name):
    bsz, _, k = y.shape
    n = w.shape[1]
    assert seq % tm == 0
    return pl.pallas_call(
        _matmul_residual_norm_kernel,
        grid=(bsz, seq // tm),
        in_specs=[
            pl.BlockSpec((1, tm, k), lambda b, i: (b, i, 0)),
            pl.BlockSpec((k, n), lambda b, i: (0, 0)),
            pl.BlockSpec((1, tm, n), lambda b, i: (b, i, 0)),
            pl.BlockSpec((1, n), lambda b, i: (0, 0)),
        ],
        out_specs=pl.BlockSpec((1, tm, n), lambda b, i: (b, i, 0)),
        out_shape=jax.ShapeDtypeStruct((bsz, seq, n), F32),
        compiler_params=_cparams("parallel", "parallel"),
        name=name,
    )(y, w, res, g)


def _conv_qkv_kernel(xm_ref, cw_ref, cb_ref, wq_ref, wk_ref, wv_ref, gq_ref, gk_ref, gv_ref, bg_ref,
                     xc_ref, q_ref, k_ref, v_ref, g_ref, pad_ref, *, seq, n_meta):
    total = seq + n_meta
    rows = CONV_ROWS
    lead = 8

    def fill(i, carry):
        r0 = pl.multiple_of(i * rows, rows)
        pad_ref[pl.ds(lead + r0, rows), :] = xm_ref[0, pl.ds(r0, rows), :].astype(F32)
        return carry

    lax.fori_loop(0, seq // rows, fill, 0)
    pad_ref[lead + seq:lead + total, :] = xm_ref[0, seq:total, :].astype(F32)
    pad_ref[0:lead, :] = pad_ref[total:lead + total, :]

    cw = cw_ref[...]
    cb = cb_ref[...]

    def taps(window, n_rows):
        acc = window[lead:lead + n_rows] * cw[A_CONV - 1:A_CONV] + cb
        for s in range(1, A_CONV):
            acc = acc + window[lead - s:lead - s + n_rows] * cw[A_CONV - 1 - s:A_CONV - s]
        return acc

    def conv(i, carry):
        r0 = pl.multiple_of(i * rows, rows)
        acc = taps(pad_ref[pl.ds(r0, rows + lead), :], rows)
        xc_ref[0, pl.ds(r0, rows), :] = (acc * _sigmoid(acc)).astype(xc_ref.dtype)
        return carry

    lax.fori_loop(0, seq // rows, conv, 0)

    window = pad_ref[seq:seq + lead + n_meta, :]
    row = lax.broadcasted_iota(jnp.int32, (n_meta, 1), 0)
    acc = window[lead:lead + n_meta] * cw[A_CONV - 1:A_CONV] + cb
    for s in range(1, A_CONV):
        shifted = window[lead - s:lead - s + n_meta]
        acc = acc + jnp.where(row >= s, shifted, 0.0) * cw[A_CONV - 1 - s:A_CONV - s]
    xc_ref[0, seq:total, :] = (acc * _sigmoid(acc)).astype(xc_ref.dtype)

    xc = xc_ref[0]
    q = jnp.dot(xc, wq_ref[0], preferred_element_type=F32).astype(BF16)
    k = jnp.dot(xc, wk_ref[0], preferred_element_type=F32).astype(BF16)
    v = jnp.dot(xm_ref[0], wv_ref[0], preferred_element_type=F32).astype(BF16)
    q_ref[0] = q
    k_ref[0] = k
    v_ref[0] = v

    @pl.when(pl.program_id(1) == 0)
    def _():
        g_ref[0] = jnp.broadcast_to(bg_ref[...], g_ref.shape[1:])

    g_ref[0] += (jnp.dot(q, gq_ref[0], preferred_element_type=F32)
                 + jnp.dot(k, gk_ref[0], preferred_element_type=F32)
                 + jnp.dot(v, gv_ref[0], preferred_element_type=F32))


def _conv_qkv(proj, cw, cb, wq, wk, wv, gq, gk, gv, bg, *, seq, n_meta):
    bsz, total, _ = proj.shape
    heads, hin, dqk = wq.shape
    dv = wv.shape[2]
    ng = gq.shape[2]
    e = heads * hin
    kern = functools.partial(_conv_qkv_kernel, seq=seq, n_meta=n_meta)
    return pl.pallas_call(
        kern,
        grid=(bsz, heads),
        in_specs=[
            pl.BlockSpec((1, total, hin), lambda b, h: (b, 0, h)),
            pl.BlockSpec((A_CONV, hin), lambda b, h: (0, h)),
            pl.BlockSpec((1, hin), lambda b, h: (0, h)),
            pl.BlockSpec((1, hin, dqk), lambda b, h: (h, 0, 0)),
            pl.BlockSpec((1, hin, dqk), lambda b, h: (h, 0, 0)),
            pl.BlockSpec((1, hin, dv), lambda b, h: (h, 0, 0)),
            pl.BlockSpec((1, dqk, ng), lambda b, h: (h, 0, 0)),
            pl.BlockSpec((1, dqk, ng), lambda b, h: (h, 0, 0)),
            pl.BlockSpec((1, dv, ng), lambda b, h: (h, 0, 0)),
            pl.BlockSpec((1, ng), lambda b, h: (0, 0)),
        ],
        out_specs=[
            pl.BlockSpec((1, total, hin), lambda b, h: (b, 0, h)),
            pl.BlockSpec((1, total, dqk), lambda b, h: (b, 0, h)),
            pl.BlockSpec((1, total, dqk), lambda b, h: (b, 0, h)),
            pl.BlockSpec((1, total, dv), lambda b, h: (b, 0, h)),
            pl.BlockSpec((1, total, ng), lambda b, h: (b, 0, 0)),
        ],
        out_shape=[
            jax.ShapeDtypeStruct((bsz, total, e), BF16),
            jax.ShapeDtypeStruct((bsz, total, heads * dqk), BF16),
            jax.ShapeDtypeStruct((bsz, total, heads * dqk), BF16),
            jax.ShapeDtypeStruct((bsz, total, heads * dv), BF16),
            jax.ShapeDtypeStruct((bsz, total, ng), F32),
        ],
        scratch_shapes=[pltpu.VMEM((total + 8, hin), F32)],
        compiler_params=_cparams("parallel", "arbitrary"),
        name="mlstm_conv_qkv",
    )(proj, cw, cb, wq, wk, wv, gq, gk, gv, bg)


def _mlstm_chunk(q, k, v, ig, f_pre, c_ref, n_ref, m_ref, scale):
    length = q.shape[0]
    ti = lax.broadcasted_iota(jnp.int32, (length, length), 0)
    si = lax.broadcasted_iota(jnp.int32, (length, length), 1)
    eye = ti == si
    b_col, b = _cumsum_row(_log_sigmoid(f_pre), ti, si)
    m_prev = m_ref[...]
    dmat = jnp.where(si <= ti, b_col - b + ig, NEG_INF)
    m_state = b_col + m_prev
    m_t = jnp.maximum(m_state, jnp.max(dmat, axis=1, keepdims=True))
    s_state = jnp.exp(m_state - m_t)
    qs = (q.astype(F32) * scale).astype(BF16)
    qk = lax.dot_general(qs, k, (((1,), (1,)), ((), ())), preferred_element_type=F32)
    w = jnp.exp(dmat - m_t) * qk
    c_old = c_ref[...]
    n_old = n_ref[...]
    num = s_state * jnp.dot(qs, c_old.astype(BF16), preferred_element_type=F32)
    num = num + jnp.dot(w.astype(BF16), v, preferred_element_type=F32)
    qn = jnp.sum(qs.astype(F32) * n_old, axis=1, keepdims=True)
    den = s_state * qn + jnp.sum(w, axis=1, keepdims=True)
    h = num / jnp.maximum(jnp.abs(den), jnp.exp(-m_t))

    g_end = b[:, length - 1:length]
    decay = g_end - b + ig
    m_new = jnp.maximum(g_end + m_prev, jnp.max(decay, axis=1, keepdims=True))
    s_old = jnp.exp(g_end + m_prev - m_new)
    w_end = _row_to_col(jnp.exp(decay - m_new), eye)
    kw = k.astype(F32) * w_end
    c_ref[...] = s_old * c_old + lax.dot_general(
        kw.astype(BF16), v, (((0,), (0,)), ((), ())), preferred_element_type=F32)
    n_ref[...] = s_old * n_old + jnp.sum(kw, axis=0, keepdims=True)
    m_ref[...] = m_new
    return h


def _mlstm_kernel(q_ref, k_ref, v_ref, o_ref, z_ref, xc_ref, hn_ref, sk_ref,
                  igr_ref, fr_ref, igm_ref, fm_ref, y_ref,
                  c_ref, n_ref, m_ref, *, seq, n_meta, chunk, scale):
    c_ref[...] = jnp.zeros_like(c_ref)
    n_ref[...] = jnp.zeros_like(n_ref)
    m_ref[...] = jnp.zeros_like(m_ref)
    hn = hn_ref[...]
    sk = sk_ref[...]

    def emit(h, r0, n_rows):
        rows = pl.ds(r0, n_rows)
        hg = h * _sigmoid(o_ref[0, rows, :].astype(F32))
        ms = jnp.mean(hg * hg, axis=-1, keepdims=True)
        hg = hg * lax.rsqrt(ms + EPS) * hn
        z = z_ref[0, rows, :].astype(F32)
        y = (hg + sk * xc_ref[0, rows, :].astype(F32)) * (z * _sigmoid(z))
        y_ref[0, rows, :] = y.astype(y_ref.dtype)

    rows = pl.ds(seq, n_meta)
    h = _mlstm_chunk(q_ref[0, rows, :], k_ref[0, rows, :], v_ref[0, rows, :],
                     igm_ref[0, 0], fm_ref[0, 0], c_ref, n_ref, m_ref, scale)
    emit(h, seq, n_meta)

    def body(c, carry):
        r0 = pl.multiple_of(c * chunk, chunk)
        rows = pl.ds(r0, chunk)
        h = _mlstm_chunk(q_ref[0, rows, :], k_ref[0, rows, :], v_ref[0, rows, :],
                         igr_ref[0, 0, pl.ds(c, 1), :], fr_ref[0, 0, pl.ds(c, 1), :],
                         c_ref, n_ref, m_ref, scale)
        emit(h, r0, chunk)
        return carry

    lax.fori_loop(0, seq // chunk, body, 0)


def _mlstm(q, k, v, proj, xc, hn, sk, ig_real, f_real, ig_meta, f_meta, *, seq, n_meta):
    bsz, total, _ = q.shape
    heads = A_HEADS
    dqk = q.shape[2] // heads
    dv = v.shape[2] // heads
    e = heads * dv
    chunk = MLSTM_CHUNK
    nc = seq // chunk
    kern = functools.partial(_mlstm_kernel, seq=seq, n_meta=n_meta, chunk=chunk, scale=dqk ** -0.5)
    o_blk = e // dv
    return pl.pallas_call(
        kern,
        grid=(bsz, heads),
        in_specs=[
            pl.BlockSpec((1, total, dqk), lambda b, h: (b, 0, h)),
            pl.BlockSpec((1, total, dqk), lambda b, h: (b, 0, h)),
            pl.BlockSpec((1, total, dv), lambda b, h: (b, 0, h)),
            pl.BlockSpec((1, total, dv), lambda b, h: (b, 0, o_blk + h)),
            pl.BlockSpec((1, total, dv), lambda b, h: (b, 0, 2 * o_blk + h)),
            pl.BlockSpec((1, total, dv), lambda b, h: (b, 0, h)),
            pl.BlockSpec((1, dv), lambda b, h: (0, h)),
            pl.BlockSpec((1, dv), lambda b, h: (0, h)),
            pl.BlockSpec((1, 1, nc, chunk), lambda b, h: (b, h, 0, 0)),
            pl.BlockSpec((1, 1, nc, chunk), lambda b, h: (b, h, 0, 0)),
            pl.BlockSpec((1, 1, 1, n_meta), lambda b, h: (b, h, 0, 0)),
            pl.BlockSpec((1, 1, 1, n_meta), lambda b, h: (b, h, 0, 0)),
        ],
        out_specs=pl.BlockSpec((1, total, dv), lambda b, h: (b, 0, h)),
        out_shape=jax.ShapeDtypeStruct((bsz, total, e), BF16),
        scratch_shapes=[
            pltpu.VMEM((dqk, dv), F32),
            pltpu.VMEM((1, dqk), F32),
            pltpu.VMEM((1, 1), F32),
        ],
        compiler_params=_cparams("parallel", "parallel"),
        name="mlstm_recurrence",
    )(q, k, v, proj, proj, xc, hn, sk, ig_real, f_real, ig_meta, f_meta)


def _split3(x):
    hi = x.astype(BF16).astype(F32)
    r1 = x - hi
    mid = r1.astype(BF16).astype(F32)
    return hi, mid, r1 - mid


def _fold_lanes(x, op):
    out = x[:, :V7X_LANES]
    for t in range(1, x.shape[1] // V7X_LANES):
        out = op(out, x[:, t * V7X_LANES:(t + 1) * V7X_LANES])
    return out


def _cumsum_lanes(x):
    length = x.shape[1]
    lane = lax.broadcasted_iota(jnp.int32, x.shape, 1)
    shift = 1
    while shift < length:
        x = x + jnp.where(lane >= shift, pltpu.roll(x, shift, axis=1), 0.0)
        shift *= 2
    return x


LOG2E = 1.4426950408889634


def _fox_gates_kernel(bf_ref, fr_ref, fm_ref, pr_ref, pm_ref, *, n_meta, qblock):
    heads, nb = fr_ref.shape[1], fr_ref.shape[2]
    lanes = pr_ref.shape[3]
    ti = lax.broadcasted_iota(jnp.int32, (n_meta, n_meta), 0)
    si = lax.broadcasted_iota(jnp.int32, (n_meta, n_meta), 1)
    bi = lax.broadcasted_iota(jnp.int32, (nb, nb), 0)
    bj = lax.broadcasted_iota(jnp.int32, (nb, nb), 1)
    lane_m = lax.broadcasted_iota(jnp.int32, (n_meta, lanes), 1)
    for h in range(heads):
        bias = bf_ref[h]
        c_meta_col, c_meta = _cumsum_row(_log_sigmoid(fm_ref[0, h] + bias) * LOG2E, ti, si)
        c_blk = _cumsum_lanes(_log_sigmoid(fr_ref[0, h] + bias) * LOG2E)
        tot_row = _col_to_row(c_blk[:, qblock - 1:qblock], bi == bj)
        offset = jnp.sum(jnp.where(bj < bi, tot_row, 0.0), axis=1, keepdims=True)
        c_rows = c_blk + (offset + c_meta[:, n_meta - 1:n_meta])
        hi, mid, lo = _split3(c_rows)
        pr_ref[0, h] = jnp.concatenate(
            [hi, mid, lo, -hi, -mid, -lo, jnp.zeros((lanes - 6 * nb, qblock), F32)], axis=0).T
        hi, mid, lo = _split3(c_meta_col)
        pm_ref[0, h] = jnp.where(
            lane_m == 0, hi, jnp.where(
                lane_m == nb, mid, jnp.where(
                    lane_m == 2 * nb, lo, jnp.where(
                        lane_m == 3 * nb, -hi, jnp.where(
                            lane_m == 4 * nb, -mid, jnp.where(lane_m == 5 * nb, -lo, 0.0))))))


def _fox_gates(bf, f_real, f_meta, *, n_meta):
    bsz, heads, nb, qblock = f_real.shape
    assert 6 * nb <= V7X_LANES
    kern = functools.partial(_fox_gates_kernel, n_meta=n_meta, qblock=qblock)
    return pl.pallas_call(
        kern,
        grid=(bsz,),
        in_specs=[
            pl.BlockSpec((heads, 1, 1), lambda b: (0, 0, 0)),
            pl.BlockSpec((1, heads, nb, qblock), lambda b: (b, 0, 0, 0)),
            pl.BlockSpec((1, heads, 1, n_meta), lambda b: (b, 0, 0, 0)),
        ],
        out_specs=[
            pl.BlockSpec((1, heads, qblock, V7X_LANES), lambda b: (b, 0, 0, 0)),
            pl.BlockSpec((1, heads, n_meta, V7X_LANES), lambda b: (b, 0, 0, 0)),
        ],
        out_shape=[
            jax.ShapeDtypeStruct((bsz, heads, qblock, V7X_LANES), F32),
            jax.ShapeDtypeStruct((bsz, heads, n_meta, V7X_LANES), F32),
        ],
        compiler_params=_cparams("parallel"),
        name="fox_gates",
    )(bf, f_real, f_meta)


def _fox_kernel(q_ref, k_ref, v_ref, z_ref, qn_ref, kn_ref, pr_ref, pm_ref, y_ref,
                ka_scr, va_scr, *, seq, n_meta, qblock, scale):
    total = seq + n_meta
    nb = seq // qblock
    hd = q_ref.shape[2]
    assert hd == V7X_LANES and 6 * nb <= hd and qblock % V7X_LANES == 0
    qg = qn_ref[...] * (scale * LOG2E)
    kg = kn_ref[...]
    pieces = pr_ref[0, 0]
    pieces_m = pm_ref[0, 0]

    def norm(x, g):
        x = x.astype(F32)
        ms = jnp.mean(x * x, axis=-1, keepdims=True)
        return (x * lax.rsqrt(ms + EPS) * g).astype(BF16)

    def finish(acc, r0, n_rows):
        rows = pl.ds(r0, n_rows)
        z = z_ref[0, rows, :].astype(F32)
        y_ref[0, rows, :] = ((acc[:, :hd] / acc[:, hd:]) * (z * _sigmoid(z))).astype(y_ref.dtype)

    lane = lax.broadcasted_iota(jnp.int32, (qblock, hd), 1)
    lane_m = lax.broadcasted_iota(jnp.int32, (n_meta, hd), 1)

    def extra(lane_ids, own, slot, *, query):
        lo_lane, ones_lo = (0, 3 * nb) if query else (3 * nb, 0)
        mine = ((lane_ids == lo_lane + slot) | (lane_ids == lo_lane + nb + slot)
                | (lane_ids == lo_lane + 2 * nb + slot))
        ones = (lane_ids >= ones_lo) & (lane_ids < ones_lo + 3 * nb)
        return jnp.where(mine, own, jnp.where(ones, 1.0, 0.0)).astype(BF16)

    def prepare_keys(i):
        r0 = i * qblock
        ka_scr[r0:r0 + qblock, 0:hd] = norm(k_ref[0, r0:r0 + qblock, :], kg)
        ka_scr[r0:r0 + qblock, hd:2 * hd] = extra(lane, pieces, i, query=False)
        va_scr[r0:r0 + qblock, 0:hd] = v_ref[0, r0:r0 + qblock, :]
        va_scr[r0:r0 + qblock, hd:2 * hd] = jnp.ones((qblock, hd), BF16)

    ka_scr[seq:total, 0:hd] = norm(k_ref[0, seq:total, :], kg)
    ka_scr[seq:total, hd:2 * hd] = extra(lane_m, pieces_m, 0, query=False)
    va_scr[seq:total, 0:hd] = v_ref[0, seq:total, :]
    va_scr[seq:total, hd:2 * hd] = jnp.ones((n_meta, hd), BF16)

    def logits(qa, r0, n_rows):
        return lax.dot_general(qa, ka_scr[r0:r0 + n_rows, :], (((1,), (1,)), ((), ())),
                               preferred_element_type=F32)

    def weighted(p, r0, n_rows):
        return jnp.dot(p.astype(BF16), va_scr[r0:r0 + n_rows, :], preferred_element_type=F32)

    qa = jnp.concatenate([norm(q_ref[0, seq:total, :], qg), extra(lane_m, pieces_m, 0, query=True)], axis=1)
    ti_m = lax.broadcasted_iota(jnp.int32, (n_meta, n_meta), 0)
    si_m = lax.broadcasted_iota(jnp.int32, (n_meta, n_meta), 1)
    s = jnp.where(si_m <= ti_m, logits(qa, seq, n_meta), NEG_INF)
    p = jnp.exp2(s - jnp.max(s, axis=1, keepdims=True))
    finish(weighted(p, seq, n_meta), seq, n_meta)

    ti = lax.broadcasted_iota(jnp.int32, (qblock, qblock), 0)
    si = lax.broadcasted_iota(jnp.int32, (qblock, qblock), 1)
    causal = si <= ti

    for j in range(nb):
        r0 = j * qblock
        prepare_keys(j)
        qa = jnp.concatenate([norm(q_ref[0, r0:r0 + qblock, :], qg), extra(lane, pieces, j, query=True)], axis=1)
        s_m = logits(qa, seq, n_meta)
        parts = []
        m_part = None
        for i in range(j + 1):
            s = logits(qa, i * qblock, qblock)
            if i == j:
                s = jnp.where(causal, s, NEG_INF)
            parts.append(s)
            folded = _fold_lanes(s, jnp.maximum)
            m_part = folded if m_part is None else jnp.maximum(m_part, folded)
        m_run = jnp.maximum(jnp.max(s_m, axis=1, keepdims=True), jnp.max(m_part, axis=1, keepdims=True))
        acc = weighted(jnp.exp2(s_m - m_run), seq, n_meta)
        for i in range(j + 1):
            acc = acc + weighted(jnp.exp2(parts[i] - m_run), i * qblock, qblock)
        finish(acc, r0, qblock)


def _fox(proj, qn, kn, pieces, pieces_meta, *, seq, n_meta):
    bsz, total, width = proj.shape
    hd = B_HEAD_DIM
    inner = width // 4
    heads = inner // hd
    qblock = FOX_QBLOCK
    kern = functools.partial(_fox_kernel, seq=seq, n_meta=n_meta, qblock=qblock, scale=hd ** -0.5)
    return pl.pallas_call(
        kern,
        grid=(bsz, heads),
        in_specs=[
            pl.BlockSpec((1, total, hd), lambda b, h: (b, 0, h)),
            pl.BlockSpec((1, total, hd), lambda b, h: (b, 0, heads + h)),
            pl.BlockSpec((1, total, hd), lambda b, h: (b, 0, 2 * heads + h)),
            pl.BlockSpec((1, total, hd), lambda b, h: (b, 0, 3 * heads + h)),
            pl.BlockSpec((1, hd), lambda b, h: (0, 0)),
            pl.BlockSpec((1, hd), lambda b, h: (0, 0)),
            pl.BlockSpec((1, 1, qblock, hd), lambda b, h: (b, h, 0, 0)),
            pl.BlockSpec((1, 1, n_meta, hd), lambda b, h: (b, h, 0, 0)),
        ],
        out_specs=pl.BlockSpec((1, total, hd), lambda b, h: (b, 0, h)),
        out_shape=jax.ShapeDtypeStruct((bsz, total, inner), BF16),
        scratch_shapes=[pltpu.VMEM((total, 2 * hd), BF16), pltpu.VMEM((total, 2 * hd), BF16)],
        compiler_params=_cparams("parallel", "parallel"),
        name="fox_attention",
    )(proj, proj, proj, proj, qn, kn, pieces, pieces_meta)


def _gate_rows(g, seq, n_meta, block):
    bsz, _, heads = g.shape
    gt = jnp.transpose(g, (0, 2, 1))
    real = gt[:, :, :seq].reshape(bsz, heads, seq // block, block)
    meta = gt[:, :, seq:].reshape(bsz, heads, 1, n_meta)
    return real, meta


def kernel(x, meta_tokens, a_norm, a_w_in, a_conv_w, a_conv_b, a_w_q, a_w_k, a_w_v, a_w_gate, a_b_gate,
           a_head_norm, a_skip, a_w_out, b_norm, b_w_in, b_b_f, b_q_norm, b_k_norm, b_w_out, final_norm):
    bsz, seq, d = x.shape
    n_meta = meta_tokens.shape[0]
    total = seq + n_meta
    m = bsz * total
    depth_a = a_norm.shape[0]
    depth_b = b_norm.shape[0]
    assert depth_a == 1 and depth_b == 1, "layer schedule: one mLSTM layer followed by one FoX layer"
    tm = total // 3
    assert total % 3 == 0 and tm % 16 == 0

    meta = jnp.broadcast_to(meta_tokens[None].astype(x.dtype), (bsz, n_meta, d))
    h0 = jnp.concatenate([x, meta], axis=1).reshape(m, d)

    heads = A_HEADS
    e = a_w_out.shape[1]
    dqk = a_w_q.shape[3]
    dv = a_w_v.shape[3]
    nqk = heads * dqk
    w_gate = a_w_gate[0].astype(BF16)
    gq = w_gate[:nqk].reshape(heads, dqk, 2 * heads)
    gk = w_gate[nqk:2 * nqk].reshape(heads, dqk, 2 * heads)
    gv = w_gate[2 * nqk:].reshape(heads, dv, 2 * heads)

    proj = _norm_matmul(h0, a_norm[0][None], a_w_in[0].astype(BF16), tm=2 * tm, tn=1024, name="mlstm_in_proj")
    proj = proj.reshape(bsz, total, 3 * e)
    xc, q, k, v, gates = _conv_qkv(
        proj, a_conv_w[0], a_conv_b[0][None], a_w_q[0].astype(BF16), a_w_k[0].astype(BF16),
        a_w_v[0].astype(BF16), gq, gk, gv, a_b_gate[0][None], seq=seq, n_meta=n_meta)
    ig_real, ig_meta = _gate_rows(gates[..., :heads], seq, n_meta, MLSTM_CHUNK)
    f_real, f_meta = _gate_rows(gates[..., heads:], seq, n_meta, MLSTM_CHUNK)
    y = _mlstm(q, k, v, proj, xc, a_head_norm[0][None], a_skip[0][None],
               ig_real, f_real, ig_meta, f_meta, seq=seq, n_meta=n_meta)
    h1 = _matmul_residual(y.reshape(m, e), a_w_out[0].astype(BF16), h0, tm=tm, tn=1024, name="mlstm_out_proj")

    inner = b_w_out.shape[1]
    bheads = inner // B_HEAD_DIM
    w_in = b_w_in[0]
    w_main = jnp.concatenate([w_in[:, :3 * inner], w_in[:, 3 * inner + bheads:]], axis=1).astype(BF16)
    w_f = jnp.pad(w_in[:, 3 * inner:3 * inner + bheads], ((0, 0), (0, V7X_LANES - bheads))).astype(BF16)
    projb, f_pre = _norm_matmul(h1, b_norm[0][None], w_main, w_f, tm=2 * tm, tn=1024, name="fox_in_proj")
    fr, fm = _gate_rows(f_pre[:, :bheads].reshape(bsz, total, bheads), seq, n_meta, FOX_QBLOCK)
    pieces, pieces_meta = _fox_gates(b_b_f[0].reshape(bheads, 1, 1), fr, fm, n_meta=n_meta)
    yb = _fox(projb.reshape(bsz, total, 4 * inner), b_q_norm[0][None], b_k_norm[0][None],
              pieces, pieces_meta, seq=seq, n_meta=n_meta)
    return _matmul_residual_norm(yb, b_w_out[0].astype(BF16), h1.reshape(bsz, total, d), final_norm[None],
                                 seq=seq, tm=512, name="fox_out_proj_norm")
```

```python
import functools

import jax
import jax.numpy as jnp
from jax import lax
from jax.experimental import pallas as pl
from jax.experimental.pallas import tpu as pltpu

F32 = jnp.float32
BF16 = jnp.bfloat16
EPS = 1e-6
NEG_INF = float("-inf")

V7X_VMEM_BYTES = 64 * 1024 * 1024
V7X_LANES = 128
BF16_SUBLANES = 16
VMEM_LIMIT = 56 * 1024 * 1024

A_HEADS = 8
A_CONV = 4
B_HEAD_DIM = 128
MLSTM_CHUNK = 256
MLSTM_HEAD_GROUP = 2
FOX_QBLOCK = 256
CONV_BLOCK = 256


def _row_tile(rows, cap):
    for t in range(cap - cap % BF16_SUBLANES, 0, -BF16_SUBLANES):
        if rows % t == 0:
            return t
    raise ValueError(f"no row tile for {rows} rows")


def _cparams(*sem, flags=None):
    return pltpu.CompilerParams(dimension_semantics=sem, vmem_limit_bytes=VMEM_LIMIT, flags=flags)


def _sigmoid(x):
    return 1.0 / (1.0 + jnp.exp(-x))


def _log_sigmoid(x):
    return jnp.minimum(x, 0.0) - jnp.log(1.0 + jnp.exp(-jnp.abs(x)))


def _row_to_col(row, eye):
    return jnp.sum(jnp.where(eye, row, 0.0), axis=1, keepdims=True)


def _col_to_row(col, eye):
    return jnp.sum(jnp.where(eye, col, 0.0), axis=0, keepdims=True)


def _cumsum_row(row, ti, si):
    col = jnp.sum(jnp.where(si <= ti, row, 0.0), axis=1, keepdims=True)
    return col, _col_to_row(col, ti == si)


def _norm_matmul_kernel(h_ref, g_ref, w_ref, o_ref, xn_ref):
    @pl.when(pl.program_id(1) == 0)
    def _():
        x = h_ref[...]
        ms = jnp.mean(x * x, axis=-1, keepdims=True)
        xn_ref[...] = (x * lax.rsqrt(ms + EPS) * g_ref[...]).astype(xn_ref.dtype)

    o_ref[...] = jnp.dot(xn_ref[...], w_ref[...], preferred_element_type=F32).astype(o_ref.dtype)


def _norm_matmul_extra_kernel(h_ref, g_ref, w_ref, wx_ref, o_ref, ox_ref, xn_ref):
    @pl.when(pl.program_id(1) == 0)
    def _():
        x = h_ref[...]
        ms = jnp.mean(x * x, axis=-1, keepdims=True)
        xn = (x * lax.rsqrt(ms + EPS) * g_ref[...]).astype(xn_ref.dtype)
        xn_ref[...] = xn
        ox_ref[...] = jnp.dot(xn, wx_ref[...], preferred_element_type=F32)

    o_ref[...] = jnp.dot(xn_ref[...], w_ref[...], preferred_element_type=F32).astype(o_ref.dtype)


def _norm_matmul(h, g, w, *, tm, tn, name):
    m, d = h.shape
    n = w.shape[1]
    assert m % tm == 0 and n % tn == 0
    return pl.pallas_call(
        _norm_matmul_kernel,
        grid=(m // tm, n // tn),
        in_specs=[
            pl.BlockSpec((tm, d), lambda i, j: (i, 0)),
            pl.BlockSpec((1, d), lambda i, j: (0, 0)),
            pl.BlockSpec((d, tn), lambda i, j: (0, j)),
        ],
        out_specs=pl.BlockSpec((tm, tn), lambda i, j: (i, j)),
        out_shape=jax.ShapeDtypeStruct((m, n), BF16),
        scratch_shapes=[pltpu.VMEM((tm, d), BF16)],
        compiler_params=_cparams("parallel", "arbitrary"),
        name=name,
    )(h, g, w)


def _norm_matmul_extra(h, g, w, wx, *, tm, tn, name):
    m, d = h.shape
    n, nx = w.shape[1], wx.shape[1]
    assert m % tm == 0 and n % tn == 0
    return pl.pallas_call(
        _norm_matmul_extra_kernel,
        grid=(m // tm, n // tn),
        in_specs=[
            pl.BlockSpec((tm, d), lambda i, j: (i, 0)),
            pl.BlockSpec((1, d), lambda i, j: (0, 0)),
            pl.BlockSpec((d, tn), lambda i, j: (0, j)),
            pl.BlockSpec((d, nx), lambda i, j: (0, 0)),
        ],
        out_specs=[pl.BlockSpec((tm, tn), lambda i, j: (i, j)), pl.BlockSpec((tm, nx), lambda i, j: (i, 0))],
        out_shape=[jax.ShapeDtypeStruct((m, n), BF16), jax.ShapeDtypeStruct((m, nx), F32)],
        scratch_shapes=[pltpu.VMEM((tm, d), BF16)],
        compiler_params=_cparams("parallel", "arbitrary"),
        name=name,
    )(h, g, w, wx)


def _matmul_residual_kernel(y_ref, w_ref, r_ref, o_ref):
    o_ref[...] = r_ref[...] + jnp.dot(y_ref[...], w_ref[...], preferred_element_type=F32)


def _matmul_residual(y, w, res, *, tm, tn, name):
    m, k = y.shape
    n = w.shape[1]
    assert m % tm == 0 and n % tn == 0
    return pl.pallas_call(
        _matmul_residual_kernel,
        grid=(m // tm, n // tn),
        in_specs=[
            pl.BlockSpec((tm, k), lambda i, j: (i, 0)),
            pl.BlockSpec((k, tn), lambda i, j: (0, j)),
            pl.BlockSpec((tm, tn), lambda i, j: (i, j)),
        ],
        out_specs=pl.BlockSpec((tm, tn), lambda i, j: (i, j)),
        out_shape=jax.ShapeDtypeStruct((m, n), F32),
        compiler_params=_cparams("parallel", "parallel"),
        name=name,
    )(y, w, res)


def _gated_out_proj_kernel(h_ref, o_ref, z_ref, xc_ref, hn_ref, sk_ref, w_ref, r_ref, out_ref, *y_scr,
                           heads_per_step, dv):
    update = None
    zero = jnp.minimum(pl.program_id(0), 0)
    for j in range(heads_per_step):
        cols = slice(j * dv, (j + 1) * dv)
        hg = h_ref[:, cols].astype(F32) * _sigmoid(o_ref[:, cols].astype(F32))
        ms = jnp.mean(hg * hg, axis=-1, keepdims=True)
        hg = hg * lax.rsqrt(ms + EPS) * hn_ref[:, cols]
        z = z_ref[:, cols].astype(F32)
        y = (hg + sk_ref[:, cols] * xc_ref[:, cols].astype(F32)) * (z * _sigmoid(z))
        y_scr[j][zero] = y.astype(BF16)
        part = jnp.dot(y_scr[j][zero], w_ref[cols, :], preferred_element_type=F32)
        update = part if update is None else update + part

    @pl.when(pl.program_id(1) == 0)
    def _():
        out_ref[...] = r_ref[...] + update

    @pl.when(pl.program_id(1) != 0)
    def _():
        out_ref[...] += update


def _gated_out_proj(h, proj, xc, hn, sk, w, res, *, tm, heads_per_step, name):
    m, e = h.shape
    d = w.shape[1]
    dv = e // A_HEADS
    kb = heads_per_step * dv
    nk = e // kb
    assert m % tm == 0 and A_HEADS % heads_per_step == 0
    kern = functools.partial(_gated_out_proj_kernel, heads_per_step=heads_per_step, dv=dv)
    return pl.pallas_call(
        kern,
        grid=(m // tm, nk),
        in_specs=[
            pl.BlockSpec((tm, kb), lambda i, k: (i, k)),
            pl.BlockSpec((tm, kb), lambda i, k: (i, nk + k)),
            pl.BlockSpec((tm, kb), lambda i, k: (i, 2 * nk + k)),
            pl.BlockSpec((tm, kb), lambda i, k: (i, k)),
            pl.BlockSpec((1, kb), lambda i, k: (0, k)),
            pl.BlockSpec((1, kb), lambda i, k: (0, k)),
            pl.BlockSpec((kb, d), lambda i, k: (k, 0)),
            pl.BlockSpec((tm, d), lambda i, k: (i, 0)),
        ],
        out_specs=pl.BlockSpec((tm, d), lambda i, k: (i, 0)),
        out_shape=jax.ShapeDtypeStruct((m, d), F32),
        scratch_shapes=[pltpu.VMEM((1, tm, dv), BF16) for _ in range(heads_per_step)],
        compiler_params=_cparams("parallel", "arbitrary"),
        name=name,
    )(h, proj, proj, xc, hn, sk, w, res)


def _matmul_residual_norm_kernel(y_ref, w_ref, r_ref, g_ref, o_ref):
    x = r_ref[0] + jnp.dot(y_ref[0], w_ref[...], preferred_element_type=F32)
    ms = jnp.mean(x * x, axis=-1, keepdims=True)
    o_ref[0] = x * lax.rsqrt(ms + EPS) * g_ref[...]


def _matmul_residual_norm(y, w, res, g, *, seq, tm, name):
    bsz, _, k = y.shape
    n = w.shape[1]
    assert seq % tm == 0
    return pl.pallas_call(
        _matmul_residual_norm_kernel,
        grid=(bsz, seq // tm),
        in_specs=[
            pl.BlockSpec((1, tm, k), lambda b, i: (b, i, 0)),
            pl.BlockSpec((k, n), lambda b, i: (0, 0)),
            pl.BlockSpec((1, tm, n), lambda b, i: (b, i, 0)),
            pl.BlockSpec((1, n), lambda b, i: (0, 0)),
        ],
        out_specs=pl.BlockSpec((1, tm, n), lambda b, i: (b, i, 0)),
        out_shape=jax.ShapeDtypeStruct((bsz, seq, n), F32),
        compiler_params=_cparams("parallel", "parallel"),
        name=name,
    )(y, w, res, g)


def _conv_qkv_kernel(xm_ref, cw_ref, cb_ref, wq_ref, wk_ref, wv_ref, gq_ref, gk_ref, gv_ref, bg_ref,
                     xc_ref, q_ref, k_ref, v_ref, g_ref, *, seq, n_meta):
    total = seq + n_meta
    blk = CONV_BLOCK
    lead = 8
    half = 16
    cw = cw_ref[...]
    cb = cb_ref[...]

    @pl.when(pl.program_id(1) == 0)
    def _():
        g_ref[0] = jnp.broadcast_to(bg_ref[...], g_ref.shape[1:])

    def project(r0, n_rows, xc):
        rows = pl.ds(r0, n_rows)
        xc_ref[0, rows, :] = xc
        q = jnp.dot(xc, wq_ref[0], preferred_element_type=F32).astype(BF16)
        k = jnp.dot(xc, wk_ref[0], preferred_element_type=F32).astype(BF16)
        v = jnp.dot(xm_ref[0, rows, :], wv_ref[0], preferred_element_type=F32).astype(BF16)
        q_ref[0, rows, :] = q
        k_ref[0, rows, :] = k
        v_ref[0, rows, :] = v
        g_ref[0, rows, :] += (jnp.dot(q, gq_ref[0], preferred_element_type=F32)
                              + jnp.dot(k, gk_ref[0], preferred_element_type=F32)
                              + jnp.dot(v, gv_ref[0], preferred_element_type=F32))

    xm_meta = xm_ref[0, seq:total, :].astype(F32)
    window = jnp.concatenate([jnp.zeros((lead, xm_meta.shape[1]), F32), xm_meta], axis=0)
    acc = cb
    for s in range(A_CONV):
        acc = acc + window[lead - s:lead - s + n_meta] * cw[A_CONV - 1 - s:A_CONV - s]
    project(seq, n_meta, (acc * _sigmoid(acc)).astype(BF16))

    for i in range(seq // blk):
        r0 = i * blk
        c0 = total - half if i == 0 else r0 - half
        ctx = xm_ref[0, c0:c0 + half, :].astype(F32)[half - lead:]
        window = jnp.concatenate([ctx, xm_ref[0, r0:r0 + blk, :].astype(F32)], axis=0)
        acc = cb
        for s in range(A_CONV):
            acc = acc + window[lead - s:lead - s + blk] * cw[A_CONV - 1 - s:A_CONV - s]
        project(r0, blk, (acc * _sigmoid(acc)).astype(BF16))


def _conv_qkv(proj, cw, cb, wq, wk, wv, gq, gk, gv, bg, *, seq, n_meta):
    bsz, total, _ = proj.shape
    heads, hin, dqk = wq.shape
    dv = wv.shape[2]
    ng = gq.shape[2]
    e = heads * hin
    kern = functools.partial(_conv_qkv_kernel, seq=seq, n_meta=n_meta)
    return pl.pallas_call(
        kern,
        grid=(bsz, heads),
        in_specs=[
            pl.BlockSpec((1, total, hin), lambda b, h: (b, 0, h)),
            pl.BlockSpec((A_CONV, hin), lambda b, h: (0, h)),
            pl.BlockSpec((1, hin), lambda b, h: (0, h)),
            pl.BlockSpec((1, hin, dqk), lambda b, h: (h, 0, 0)),
            pl.BlockSpec((1, hin, dqk), lambda b, h: (h, 0, 0)),
            pl.BlockSpec((1, hin, dv), lambda b, h: (h, 0, 0)),
            pl.BlockSpec((1, dqk, ng), lambda b, h: (h, 0, 0)),
            pl.BlockSpec((1, dqk, ng), lambda b, h: (h, 0, 0)),
            pl.BlockSpec((1, dv, ng), lambda b, h: (h, 0, 0)),
            pl.BlockSpec((1, ng), lambda b, h: (0, 0)),
        ],
        out_specs=[
            pl.BlockSpec((1, total, hin), lambda b, h: (b, 0, h)),
            pl.BlockSpec((1, total, dqk), lambda b, h: (b, 0, h)),
            pl.BlockSpec((1, total, dqk), lambda b, h: (b, 0, h)),
            pl.BlockSpec((1, total, dv), lambda b, h: (b, 0, h)),
            pl.BlockSpec((1, total, ng), lambda b, h: (b, 0, 0)),
        ],
        out_shape=[
            jax.ShapeDtypeStruct((bsz, total, e), BF16),
            jax.ShapeDtypeStruct((bsz, total, heads * dqk), BF16),
            jax.ShapeDtypeStruct((bsz, total, heads * dqk), BF16),
            jax.ShapeDtypeStruct((bsz, total, heads * dv), BF16),
            jax.ShapeDtypeStruct((bsz, total, ng), F32),
        ],
        compiler_params=_cparams("parallel", "arbitrary"),
        name="mlstm_conv_qkv",
    )(proj, cw, cb, wq, wk, wv, gq, gk, gv, bg)


def _mlstm_chunk(q, k, v, ig, f_pre, c_ref, n_ref, m_ref, scale, emit):
    length = q.shape[0]
    ti = lax.broadcasted_iota(jnp.int32, (length, length), 0)
    si = lax.broadcasted_iota(jnp.int32, (length, length), 1)
    eye = ti == si
    b_col, b = _cumsum_row(_log_sigmoid(f_pre), ti, si)
    m_prev = m_ref[...]
    dmat = jnp.where(si <= ti, b_col - b + ig, NEG_INF)
    m_state = b_col + m_prev
    m_t = jnp.maximum(m_state, jnp.max(dmat, axis=1, keepdims=True))
    s_state = jnp.exp(m_state - m_t)
    yield
    qs = (q.astype(F32) * scale).astype(BF16)
    qk = lax.dot_general(qs, k, (((1,), (1,)), ((), ())), preferred_element_type=F32)
    yield
    w = jnp.exp(dmat - m_t) * qk
    yield
    c_old = c_ref[...]
    n_old = n_ref[...]
    num = s_state * jnp.dot(qs, c_old.astype(BF16), preferred_element_type=F32)
    num = num + jnp.dot(w.astype(BF16), v, preferred_element_type=F32)
    yield
    qn = jnp.sum(qs.astype(F32) * n_old, axis=1, keepdims=True)
    den = s_state * qn + jnp.sum(w, axis=1, keepdims=True)
    emit(num / jnp.maximum(jnp.abs(den), jnp.exp(-m_t)))
    yield
    g_end = b[:, length - 1:length]
    decay = g_end - b + ig
    m_new = jnp.maximum(g_end + m_prev, jnp.max(decay, axis=1, keepdims=True))
    s_old = jnp.exp(g_end + m_prev - m_new)
    w_end = _row_to_col(jnp.exp(decay - m_new), eye)
    kw = k.astype(F32) * w_end
    yield
    c_ref[...] = s_old * c_old + lax.dot_general(
        kw.astype(BF16), v, (((0,), (0,)), ((), ())), preferred_element_type=F32)
    n_ref[...] = s_old * n_old + jnp.sum(kw, axis=0, keepdims=True)
    m_ref[...] = m_new


def _mlstm_kernel(q_ref, k_ref, v_ref, igr_ref, fr_ref, igm_ref, fm_ref, h_ref,
                  c_ref, n_ref, m_ref, *, seq, n_meta, chunk, scale, group, dqk, dv):
    c_ref[...] = jnp.zeros_like(c_ref)
    n_ref[...] = jnp.zeros_like(n_ref)
    m_ref[...] = jnp.zeros_like(m_ref)

    def stages(rows, g, ig, f_pre):
        qc = slice(g * dqk, (g + 1) * dqk)
        vc = slice(g * dv, (g + 1) * dv)

        def emit(h):
            h_ref[0, rows, vc] = h.astype(h_ref.dtype)

        return _mlstm_chunk(q_ref[0, rows, qc], k_ref[0, rows, qc], v_ref[0, rows, vc], ig, f_pre,
                            c_ref.at[g], n_ref.at[g], m_ref.at[g], scale, emit)

    def side_by_side(gens):
        gens = list(gens)
        while gens:
            gens = [gen for gen in gens if next(gen, "done") != "done"]

    side_by_side(stages(pl.ds(seq, n_meta), g, igm_ref[0, g], fm_ref[0, g]) for g in range(group))
    for c in range(seq // chunk):
        side_by_side(stages(pl.ds(c * chunk, chunk), g, igr_ref[0, g, c:c + 1, :], fr_ref[0, g, c:c + 1, :])
                     for g in range(group))


def _mlstm(q, k, v, ig_real, f_real, ig_meta, f_meta, *, seq, n_meta):
    bsz, total, _ = q.shape
    heads = A_HEADS
    dqk = q.shape[2] // heads
    dv = v.shape[2] // heads
    e = heads * dv
    chunk = MLSTM_CHUNK
    nc = seq // chunk
    group = MLSTM_HEAD_GROUP
    kern = functools.partial(_mlstm_kernel, seq=seq, n_meta=n_meta, chunk=chunk, scale=dqk ** -0.5,
                             group=group, dqk=dqk, dv=dv)
    return pl.pallas_call(
        kern,
        grid=(bsz, heads // group),
        in_specs=[
            pl.BlockSpec((1, total, group * dqk), lambda b, h: (b, 0, h)),
            pl.BlockSpec((1, total, group * dqk), lambda b, h: (b, 0, h)),
            pl.BlockSpec((1, total, group * dv), lambda b, h: (b, 0, h)),
            pl.BlockSpec((1, group, nc, chunk), lambda b, h: (b, h, 0, 0)),
            pl.BlockSpec((1, group, nc, chunk), lambda b, h: (b, h, 0, 0)),
            pl.BlockSpec((1, group, 1, n_meta), lambda b, h: (b, h, 0, 0)),
            pl.BlockSpec((1, group, 1, n_meta), lambda b, h: (b, h, 0, 0)),
        ],
        out_specs=pl.BlockSpec((1, total, group * dv), lambda b, h: (b, 0, h)),
        out_shape=jax.ShapeDtypeStruct((bsz, total, e), BF16),
        scratch_shapes=[
            pltpu.VMEM((group, dqk, dv), F32),
            pltpu.VMEM((group, 1, dqk), F32),
            pltpu.VMEM((group, 1, 1), F32),
        ],
        compiler_params=_cparams("parallel", "parallel"),
        name="mlstm_recurrence",
    )(q, k, v, ig_real, f_real, ig_meta, f_meta)


def _split3(x):
    hi = x.astype(BF16).astype(F32)
    r1 = x - hi
    mid = r1.astype(BF16).astype(F32)
    return hi, mid, r1 - mid


def _fold_lanes(x, op):
    out = x[:, :V7X_LANES]
    for t in range(1, x.shape[1] // V7X_LANES):
        out = op(out, x[:, t * V7X_LANES:(t + 1) * V7X_LANES])
    return out


def _cumsum_lanes(x):
    length = x.shape[1]
    lane = lax.broadcasted_iota(jnp.int32, x.shape, 1)
    shift = 1
    while shift < length:
        x = x + jnp.where(lane >= shift, pltpu.roll(x, shift, axis=1), 0.0)
        shift *= 2
    return x


LOG2E = 1.4426950408889634


def _fox_gates_kernel(bf_ref, fr_ref, fm_ref, pr_ref, pm_ref, *, n_meta, qblock):
    heads, nb = fr_ref.shape[1], fr_ref.shape[2]
    lanes = pr_ref.shape[3]
    ti = lax.broadcasted_iota(jnp.int32, (n_meta, n_meta), 0)
    si = lax.broadcasted_iota(jnp.int32, (n_meta, n_meta), 1)
    bi = lax.broadcasted_iota(jnp.int32, (nb, nb), 0)
    bj = lax.broadcasted_iota(jnp.int32, (nb, nb), 1)
    lane_m = lax.broadcasted_iota(jnp.int32, (n_meta, lanes), 1)
    for h in range(heads):
        bias = bf_ref[h]
        c_meta_col, c_meta = _cumsum_row(_log_sigmoid(fm_ref[0, h] + bias) * LOG2E, ti, si)
        c_blk = _cumsum_lanes(_log_sigmoid(fr_ref[0, h] + bias) * LOG2E)
        tot_row = _col_to_row(c_blk[:, qblock - 1:qblock], bi == bj)
        offset = jnp.sum(jnp.where(bj < bi, tot_row, 0.0), axis=1, keepdims=True)
        c_rows = c_blk + (offset + c_meta[:, n_meta - 1:n_meta])
        hi, mid, lo = _split3(c_rows)
        pr_ref[0, h] = jnp.concatenate(
            [hi, mid, lo, -hi, -mid, -lo, jnp.zeros((lanes - 6 * nb, qblock), F32)], axis=0).T
        hi, mid, lo = _split3(c_meta_col)
        pm_ref[0, h] = jnp.where(
            lane_m == 0, hi, jnp.where(
                lane_m == nb, mid, jnp.where(
                    lane_m == 2 * nb, lo, jnp.where(
                        lane_m == 3 * nb, -hi, jnp.where(
                            lane_m == 4 * nb, -mid, jnp.where(lane_m == 5 * nb, -lo, 0.0))))))


def _fox_gates(bf, f_real, f_meta, *, n_meta):
    bsz, heads, nb, qblock = f_real.shape
    assert 6 * nb <= V7X_LANES
    kern = functools.partial(_fox_gates_kernel, n_meta=n_meta, qblock=qblock)
    return pl.pallas_call(
        kern,
        grid=(bsz,),
        in_specs=[
            pl.BlockSpec((heads, 1, 1), lambda b: (0, 0, 0)),
            pl.BlockSpec((1, heads, nb, qblock), lambda b: (b, 0, 0, 0)),
            pl.BlockSpec((1, heads, 1, n_meta), lambda b: (b, 0, 0, 0)),
        ],
        out_specs=[
            pl.BlockSpec((1, heads, qblock, V7X_LANES), lambda b: (b, 0, 0, 0)),
            pl.BlockSpec((1, heads, n_meta, V7X_LANES), lambda b: (b, 0, 0, 0)),
        ],
        out_shape=[
            jax.ShapeDtypeStruct((bsz, heads, qblock, V7X_LANES), F32),
            jax.ShapeDtypeStruct((bsz, heads, n_meta, V7X_LANES), F32),
        ],
        compiler_params=_cparams("parallel"),
        name="fox_gates",
    )(bf, f_real, f_meta)


def _fox_kernel(q_ref, k_ref, v_ref, z_ref, qn_ref, kn_ref, pr_ref, pm_ref, y_ref,
                ka_scr, va_scr, *, seq, n_meta, qblock, scale):
    total = seq + n_meta
    nb = seq // qblock
    hd = q_ref.shape[2]
    assert hd == V7X_LANES and 6 * nb <= hd and qblock % V7X_LANES == 0
    qg = qn_ref[...] * (scale * LOG2E)
    kg = kn_ref[...]
    pieces = pr_ref[0, 0]
    pieces_m = pm_ref[0, 0]

    def norm(x, g):
        x = x.astype(F32)
        ms = jnp.mean(x * x, axis=-1, keepdims=True)
        return (x * lax.rsqrt(ms + EPS) * g).astype(BF16)

    def finish(acc, r0, n_rows):
        rows = pl.ds(r0, n_rows)
        z = z_ref[0, rows, :].astype(F32)
        y_ref[0, rows, :] = ((acc[:, :hd] / acc[:, hd:]) * (z * _sigmoid(z))).astype(y_ref.dtype)

    lane = lax.broadcasted_iota(jnp.int32, (qblock, hd), 1)
    lane_m = lax.broadcasted_iota(jnp.int32, (n_meta, hd), 1)

    def extra(lane_ids, own, slot, *, query):
        lo_lane, ones_lo = (0, 3 * nb) if query else (3 * nb, 0)
        mine = ((lane_ids == lo_lane + slot) | (lane_ids == lo_lane + nb + slot)
                | (lane_ids == lo_lane + 2 * nb + slot))
        ones = (lane_ids >= ones_lo) & (lane_ids < ones_lo + 3 * nb)
        return jnp.where(mine, own, jnp.where(ones, 1.0, 0.0)).astype(BF16)

    def prepare_keys(i):
        r0 = i * qblock
        ka_scr[r0:r0 + qblock, 0:hd] = norm(k_ref[0, r0:r0 + qblock, :], kg)
        ka_scr[r0:r0 + qblock, hd:2 * hd] = extra(lane, pieces, i, query=False)
        va_scr[r0:r0 + qblock, 0:hd] = v_ref[0, r0:r0 + qblock, :]
        va_scr[r0:r0 + qblock, hd:2 * hd] = jnp.ones((qblock, hd), BF16)

    ka_scr[seq:total, 0:hd] = norm(k_ref[0, seq:total, :], kg)
    ka_scr[seq:total, hd:2 * hd] = extra(lane_m, pieces_m, 0, query=False)
    va_scr[seq:total, 0:hd] = v_ref[0, seq:total, :]
    va_scr[seq:total, hd:2 * hd] = jnp.ones((n_meta, hd), BF16)

    def logits(qa, r0, n_rows):
        return lax.dot_general(qa, ka_scr[r0:r0 + n_rows, :], (((1,), (1,)), ((), ())),
                               preferred_element_type=F32)

    def weighted(p, r0, n_rows):
        return jnp.dot(p.astype(BF16), va_scr[r0:r0 + n_rows, :], preferred_element_type=F32)

    qa = jnp.concatenate([norm(q_ref[0, seq:total, :], qg), extra(lane_m, pieces_m, 0, query=True)], axis=1)
    ti_m = lax.broadcasted_iota(jnp.int32, (n_meta, n_meta), 0)
    si_m = lax.broadcasted_iota(jnp.int32, (n_meta, n_meta), 1)
    s = jnp.where(si_m <= ti_m, logits(qa, seq, n_meta), NEG_INF)
    p = jnp.exp2(s - jnp.max(s, axis=1, keepdims=True))
    finish(weighted(p, seq, n_meta), seq, n_meta)

    ti = lax.broadcasted_iota(jnp.int32, (qblock, qblock), 0)
    si = lax.broadcasted_iota(jnp.int32, (qblock, qblock), 1)
    causal = si <= ti

    for j in range(nb):
        r0 = j * qblock
        prepare_keys(j)
        qa = jnp.concatenate([norm(q_ref[0, r0:r0 + qblock, :], qg), extra(lane, pieces, j, query=True)], axis=1)
        s_m = logits(qa, seq, n_meta)
        parts = []
        m_part = None
        for i in range(j + 1):
            s = logits(qa, i * qblock, qblock)
            if i == j:
                s = jnp.where(causal, s, NEG_INF)
            parts.append(s)
            folded = _fold_lanes(s, jnp.maximum)
            m_part = folded if m_part is None else jnp.maximum(m_part, folded)
        m_run = jnp.maximum(jnp.max(s_m, axis=1, keepdims=True), jnp.max(m_part, axis=1, keepdims=True))
        acc = weighted(jnp.exp2(s_m - m_run), seq, n_meta)
        for i in range(j + 1):
            acc = acc + weighted(jnp.exp2(parts[i] - m_run), i * qblock, qblock)
        finish(acc, r0, qblock)


def _fox(proj, qn, kn, pieces, pieces_meta, *, seq, n_meta):
    bsz, total, width = proj.shape
    hd = B_HEAD_DIM
    inner = width // 4
    heads = inner // hd
    qblock = FOX_QBLOCK
    kern = functools.partial(_fox_kernel, seq=seq, n_meta=n_meta, qblock=qblock, scale=hd ** -0.5)
    return pl.pallas_call(
        kern,
        grid=(bsz, heads),
        in_specs=[
            pl.BlockSpec((1, total, hd), lambda b, h: (b, 0, h)),
            pl.BlockSpec((1, total, hd), lambda b, h: (b, 0, heads + h)),
            pl.BlockSpec((1, total, hd), lambda b, h: (b, 0, 2 * heads + h)),
            pl.BlockSpec((1, total, hd), lambda b, h: (b, 0, 3 * heads + h)),
            pl.BlockSpec((1, hd), lambda b, h: (0, 0)),
            pl.BlockSpec((1, hd), lambda b, h: (0, 0)),
            pl.BlockSpec((1, 1, qblock, hd), lambda b, h: (b, h, 0, 0)),
            pl.BlockSpec((1, 1, n_meta, hd), lambda b, h: (b, h, 0, 0)),
        ],
        out_specs=pl.BlockSpec((1, total, hd), lambda b, h: (b, 0, h)),
        out_shape=jax.ShapeDtypeStruct((bsz, total, inner), BF16),
        scratch_shapes=[pltpu.VMEM((total, 2 * hd), BF16), pltpu.VMEM((total, 2 * hd), BF16)],
        compiler_params=_cparams("parallel", "parallel"),
        name="fox_attention",
    )(proj, proj, proj, proj, qn, kn, pieces, pieces_meta)


def _gate_rows(g, seq, n_meta, block):
    bsz, _, heads = g.shape
    gt = jnp.transpose(g, (0, 2, 1))
    real = gt[:, :, :seq].reshape(bsz, heads, seq // block, block)
    meta = gt[:, :, seq:].reshape(bsz, heads, 1, n_meta)
    return real, meta


def kernel(x, meta_tokens, a_norm, a_w_in, a_conv_w, a_conv_b, a_w_q, a_w_k, a_w_v, a_w_gate, a_b_gate,
           a_head_norm, a_skip, a_w_out, b_norm, b_w_in, b_b_f, b_q_norm, b_k_norm, b_w_out, final_norm):
    bsz, seq, d = x.shape
    n_meta = meta_tokens.shape[0]
    total = seq + n_meta
    m = bsz * total
    depth_a = a_norm.shape[0]
    depth_b = b_norm.shape[0]
    assert depth_a == 1 and depth_b == 1, "layer schedule: one mLSTM layer followed by one FoX layer"
    tm = total // 3
    assert total % 3 == 0 and tm % 16 == 0

    meta = jnp.broadcast_to(meta_tokens[None].astype(x.dtype), (bsz, n_meta, d))
    h0 = jnp.concatenate([x, meta], axis=1).reshape(m, d)

    heads = A_HEADS
    e = a_w_out.shape[1]
    dqk = a_w_q.shape[3]
    dv = a_w_v.shape[3]
    nqk = heads * dqk
    w_gate = a_w_gate[0].astype(BF16)
    gq = w_gate[:nqk].reshape(heads, dqk, 2 * heads)
    gk = w_gate[nqk:2 * nqk].reshape(heads, dqk, 2 * heads)
    gv = w_gate[2 * nqk:].reshape(heads, dv, 2 * heads)

    proj = _norm_matmul(h0, a_norm[0][None], a_w_in[0].astype(BF16), tm=2 * tm, tn=1024, name="mlstm_in_proj")
    proj = proj.reshape(bsz, total, 3 * e)
    xc, q, k, v, gates = _conv_qkv(
        proj, a_conv_w[0], a_conv_b[0][None], a_w_q[0].astype(BF16), a_w_k[0].astype(BF16),
        a_w_v[0].astype(BF16), gq, gk, gv, a_b_gate[0][None], seq=seq, n_meta=n_meta)
    ig_real, ig_meta = _gate_rows(gates[..., :heads], seq, n_meta, MLSTM_CHUNK)
    f_real, f_meta = _gate_rows(gates[..., heads:], seq, n_meta, MLSTM_CHUNK)
    hs = _mlstm(q, k, v, ig_real, f_real, ig_meta, f_meta, seq=seq, n_meta=n_meta)
    h1 = _gated_out_proj(hs.reshape(m, e), proj.reshape(m, 3 * e), xc.reshape(m, e), a_head_norm[0][None],
                         a_skip[0][None], a_w_out[0].astype(BF16), h0, tm=_row_tile(m, 512), heads_per_step=4,
                         name="mlstm_out_proj")

    inner = b_w_out.shape[1]
    bheads = inner // B_HEAD_DIM
    w_in = b_w_in[0]
    w_main = jnp.concatenate([w_in[:, :3 * inner].astype(BF16), w_in[:, 3 * inner + bheads:].astype(BF16)], axis=1)
    w_f = jnp.pad(w_in[:, 3 * inner:3 * inner + bheads].astype(BF16), ((0, 0), (0, V7X_LANES - bheads)))
    projb, f_pre = _norm_matmul_extra(h1, b_norm[0][None], w_main, w_f, tm=2 * tm, tn=1024, name="fox_in_proj")
    fr, fm = _gate_rows(f_pre[:, :bheads].reshape(bsz, total, bheads), seq, n_meta, FOX_QBLOCK)
    pieces, pieces_meta = _fox_gates(b_b_f[0].reshape(bheads, 1, 1), fr, fm, n_meta=n_meta)
    yb = _fox(projb.reshape(bsz, total, 4 * inner), b_q_norm[0][None], b_k_norm[0][None],
              pieces, pieces_meta, seq=seq, n_meta=n_meta)
    return _matmul_residual_norm(yb, b_w_out[0].astype(BF16), h1.reshape(bsz, total, d), final_norm[None],
                                 seq=seq, tm=512, name="fox_out_proj_norm")
```

```python
import functools

import jax
import jax.numpy as jnp
from jax import lax
from jax.experimental import pallas as pl
from jax.experimental.pallas import tpu as pltpu

F32 = jnp.float32
BF16 = jnp.bfloat16
EPS = 1e-6
NEG_INF = float("-inf")

V7X_VMEM_BYTES = 64 * 1024 * 1024
V7X_LANES = 128
BF16_SUBLANES = 16
VMEM_LIMIT = 56 * 1024 * 1024

A_HEADS = 8
A_CONV = 4
B_HEAD_DIM = 128
MLSTM_CHUNK = 256
MLSTM_HEAD_GROUP = 2
FOX_QBLOCK = 256
FOX_HEAD_GROUP = 4
CONV_BLOCK = 256


def _row_tile(rows, cap):
    for t in range(cap - cap % BF16_SUBLANES, 0, -BF16_SUBLANES):
        if rows % t == 0:
            return t
    raise ValueError(f"no row tile for {rows} rows")


def _cparams(*sem, flags=None):
    return pltpu.CompilerParams(dimension_semantics=sem, vmem_limit_bytes=VMEM_LIMIT, flags=flags)


LOG2E = 1.4426950408889634


def _sigmoid(x):
    return 1.0 / (1.0 + jnp.exp2(x * -LOG2E))


def _log_sigmoid(x):
    return jnp.minimum(x, 0.0) - jnp.log(1.0 + jnp.exp(-jnp.abs(x)))


def _row_to_col(row, eye):
    return jnp.sum(jnp.where(eye, row, 0.0), axis=1, keepdims=True)


def _col_to_row(col, eye):
    return jnp.sum(jnp.where(eye, col, 0.0), axis=0, keepdims=True)


def _cumsum_row(row, ti, si):
    col = jnp.sum(jnp.where(si <= ti, row, 0.0), axis=1, keepdims=True)
    return col, _col_to_row(col, ti == si)


def _norm_matmul_kernel(h_ref, g_ref, w_ref, o_ref, xn_ref):
    @pl.when(pl.program_id(1) == 0)
    def _():
        x = h_ref[...]
        ms = jnp.mean(x * x, axis=-1, keepdims=True)
        xn_ref[...] = (x * lax.rsqrt(ms + EPS) * g_ref[...]).astype(xn_ref.dtype)

    o_ref[...] = jnp.dot(xn_ref[...], w_ref[...], preferred_element_type=F32).astype(o_ref.dtype)


def _norm_matmul_extra_kernel(h_ref, g_ref, w_ref, wx_ref, o_ref, ox_ref, xn_ref):
    @pl.when(pl.program_id(1) == 0)
    def _():
        x = h_ref[...]
        ms = jnp.mean(x * x, axis=-1, keepdims=True)
        xn = (x * lax.rsqrt(ms + EPS) * g_ref[...]).astype(xn_ref.dtype)
        xn_ref[...] = xn
        ox_ref[...] = jnp.dot(xn, wx_ref[...], preferred_element_type=F32)

    o_ref[...] = jnp.dot(xn_ref[...], w_ref[...], preferred_element_type=F32).astype(o_ref.dtype)


def _norm_matmul(h, g, w, *, tm, tn, name):
    m, d = h.shape
    n = w.shape[1]
    assert m % tm == 0 and n % tn == 0
    return pl.pallas_call(
        _norm_matmul_kernel,
        grid=(m // tm, n // tn),
        in_specs=[
            pl.BlockSpec((tm, d), lambda i, j: (i, 0)),
            pl.BlockSpec((1, d), lambda i, j: (0, 0)),
            pl.BlockSpec((d, tn), lambda i, j: (0, j)),
        ],
        out_specs=pl.BlockSpec((tm, tn), lambda i, j: (i, j)),
        out_shape=jax.ShapeDtypeStruct((m, n), BF16),
        scratch_shapes=[pltpu.VMEM((tm, d), BF16)],
        compiler_params=_cparams("parallel", "arbitrary"),
        name=name,
    )(h, g, w)


def _norm_matmul_extra(h, g, w, wx, *, tm, tn, name):
    m, d = h.shape
    n, nx = w.shape[1], wx.shape[1]
    assert m % tm == 0 and n % tn == 0
    return pl.pallas_call(
        _norm_matmul_extra_kernel,
        grid=(m // tm, n // tn),
        in_specs=[
            pl.BlockSpec((tm, d), lambda i, j: (i, 0)),
            pl.BlockSpec((1, d), lambda i, j: (0, 0)),
            pl.BlockSpec((d, tn), lambda i, j: (0, j)),
            pl.BlockSpec((d, nx), lambda i, j: (0, 0)),
        ],
        out_specs=[pl.BlockSpec((tm, tn), lambda i, j: (i, j)), pl.BlockSpec((tm, nx), lambda i, j: (i, 0))],
        out_shape=[jax.ShapeDtypeStruct((m, n), BF16), jax.ShapeDtypeStruct((m, nx), F32)],
        scratch_shapes=[pltpu.VMEM((tm, d), BF16)],
        compiler_params=_cparams("parallel", "arbitrary"),
        name=name,
    )(h, g, w, wx)


def _matmul_residual_kernel(y_ref, w_ref, r_ref, o_ref):
    o_ref[...] = r_ref[...] + jnp.dot(y_ref[...], w_ref[...], preferred_element_type=F32)


def _matmul_residual(y, w, res, *, tm, tn, name):
    m, k = y.shape
    n = w.shape[1]
    assert m % tm == 0 and n % tn == 0
    return pl.pallas_call(
        _matmul_residual_kernel,
        grid=(m // tm, n // tn),
        in_specs=[
            pl.BlockSpec((tm, k), lambda i, j: (i, 0)),
            pl.BlockSpec((k, tn), lambda i, j: (0, j)),
            pl.BlockSpec((tm, tn), lambda i, j: (i, j)),
        ],
        out_specs=pl.BlockSpec((tm, tn), lambda i, j: (i, j)),
        out_shape=jax.ShapeDtypeStruct((m, n), F32),
        compiler_params=_cparams("parallel", "parallel"),
        name=name,
    )(y, w, res)


def _gated_out_proj_kernel(h_ref, o_ref, z_ref, xc_ref, hn_ref, sk_ref, w_ref, r_ref, out_ref, *y_scr,
                           heads_per_step, dv):
    def gate(j):
        cols = slice(j * dv, (j + 1) * dv)
        hg = h_ref[:, cols].astype(F32) * _sigmoid(o_ref[:, cols].astype(F32))
        ms = jnp.mean(hg * hg, axis=-1, keepdims=True)
        hg = hg * lax.rsqrt(ms + EPS) * hn_ref[:, cols]
        z = z_ref[:, cols].astype(F32)
        y = (hg + sk_ref[:, cols] * xc_ref[:, cols].astype(F32)) * (z * _sigmoid(z))
        y_scr[j][...] = y.astype(BF16)

    update = None
    gate(0)
    for j in range(heads_per_step):
        if j + 1 < heads_per_step:
            gate(j + 1)
        part = jnp.dot(y_scr[j][...], w_ref[j * dv:(j + 1) * dv, :], preferred_element_type=F32)
        update = part if update is None else update + part

    @pl.when(pl.program_id(1) == 0)
    def _():
        out_ref[...] = r_ref[...] + update

    @pl.when(pl.program_id(1) != 0)
    def _():
        out_ref[...] += update


def _gated_out_proj(h, proj, xc, hn, sk, w, res, *, tm, heads_per_step, name):
    m, e = h.shape
    d = w.shape[1]
    dv = e // A_HEADS
    kb = heads_per_step * dv
    nk = e // kb
    assert m % tm == 0 and A_HEADS % heads_per_step == 0
    kern = functools.partial(_gated_out_proj_kernel, heads_per_step=heads_per_step, dv=dv)
    return pl.pallas_call(
        kern,
        grid=(m // tm, nk),
        in_specs=[
            pl.BlockSpec((tm, kb), lambda i, k: (i, k)),
            pl.BlockSpec((tm, kb), lambda i, k: (i, nk + k)),
            pl.BlockSpec((tm, kb), lambda i, k: (i, 2 * nk + k)),
            pl.BlockSpec((tm, kb), lambda i, k: (i, k)),
            pl.BlockSpec((1, kb), lambda i, k: (0, k)),
            pl.BlockSpec((1, kb), lambda i, k: (0, k)),
            pl.BlockSpec((kb, d), lambda i, k: (k, 0)),
            pl.BlockSpec((tm, d), lambda i, k: (i, 0)),
        ],
        out_specs=pl.BlockSpec((tm, d), lambda i, k: (i, 0)),
        out_shape=jax.ShapeDtypeStruct((m, d), F32),
        scratch_shapes=[pltpu.VMEM((tm, dv), BF16) for _ in range(heads_per_step)],
        compiler_params=_cparams("parallel", "arbitrary"),
        name=name,
    )(h, proj, proj, xc, hn, sk, w, res)


def _matmul_residual_norm_kernel(y_ref, w_ref, r_ref, g_ref, o_ref):
    x = r_ref[0] + jnp.dot(y_ref[0], w_ref[...], preferred_element_type=F32)
    ms = jnp.mean(x * x, axis=-1, keepdims=True)
    o_ref[0] = x * lax.rsqrt(ms + EPS) * g_ref[...]


def _matmul_residual_norm(y, w, res, g, *, seq, tm, name):
    bsz, _, k = y.shape
    n = w.shape[1]
    assert seq % tm == 0
    return pl.pallas_call(
        _matmul_residual_norm_kernel,
        grid=(bsz, seq // tm),
        in_specs=[
            pl.BlockSpec((1, tm, k), lambda b, i: (b, i, 0)),
            pl.BlockSpec((k, n), lambda b, i: (0, 0)),
            pl.BlockSpec((1, tm, n), lambda b, i: (b, i, 0)),
            pl.BlockSpec((1, n), lambda b, i: (0, 0)),
        ],
        out_specs=pl.BlockSpec((1, tm, n), lambda b, i: (b, i, 0)),
        out_shape=jax.ShapeDtypeStruct((bsz, seq, n), F32),
        compiler_params=_cparams("parallel", "parallel"),
        name=name,
    )(y, w, res, g)


def _conv_qkv_kernel(xm_ref, cw_ref, cb_ref, wq_ref, wk_ref, wv_ref, gq_ref, gk_ref, gv_ref, bg_ref,
                     xc_ref, q_ref, k_ref, v_ref, g_ref, *, seq, n_meta):
    total = seq + n_meta
    blk = CONV_BLOCK
    lead = 8
    half = 16
    cw = cw_ref[...]
    cb = cb_ref[...]

    @pl.when(pl.program_id(1) == 0)
    def _():
        g_ref[0] = jnp.broadcast_to(bg_ref[...], g_ref.shape[1:])

    def project(r0, n_rows, xc):
        rows = pl.ds(r0, n_rows)
        xc_ref[0, rows, :] = xc
        q = jnp.dot(xc, wq_ref[0], preferred_element_type=F32).astype(BF16)
        k = jnp.dot(xc, wk_ref[0], preferred_element_type=F32).astype(BF16)
        v = jnp.dot(xm_ref[0, rows, :], wv_ref[0], preferred_element_type=F32).astype(BF16)
        q_ref[0, rows, :] = q
        k_ref[0, rows, :] = k
        v_ref[0, rows, :] = v
        g_ref[0, rows, :] += (jnp.dot(q, gq_ref[0], preferred_element_type=F32)
                              + jnp.dot(k, gk_ref[0], preferred_element_type=F32)
                              + jnp.dot(v, gv_ref[0], preferred_element_type=F32))

    xm_meta = xm_ref[0, seq:total, :].astype(F32)
    window = jnp.concatenate([jnp.zeros((lead, xm_meta.shape[1]), F32), xm_meta], axis=0)
    acc = cb
    for s in range(A_CONV):
        acc = acc + window[lead - s:lead - s + n_meta] * cw[A_CONV - 1 - s:A_CONV - s]
    project(seq, n_meta, (acc * _sigmoid(acc)).astype(BF16))

    for i in range(seq // blk):
        r0 = i * blk
        c0 = total - half if i == 0 else r0 - half
        ctx = xm_ref[0, c0:c0 + half, :].astype(F32)[half - lead:]
        window = jnp.concatenate([ctx, xm_ref[0, r0:r0 + blk, :].astype(F32)], axis=0)
        acc = cb
        for s in range(A_CONV):
            acc = acc + window[lead - s:lead - s + blk] * cw[A_CONV - 1 - s:A_CONV - s]
        project(r0, blk, (acc * _sigmoid(acc)).astype(BF16))


def _conv_qkv(proj, cw, cb, wq, wk, wv, gq, gk, gv, bg, *, seq, n_meta):
    bsz, total, _ = proj.shape
    heads, hin, dqk = wq.shape
    dv = wv.shape[2]
    ng = gq.shape[2]
    e = heads * hin
    kern = functools.partial(_conv_qkv_kernel, seq=seq, n_meta=n_meta)
    return pl.pallas_call(
        kern,
        grid=(bsz, heads),
        in_specs=[
            pl.BlockSpec((1, total, hin), lambda b, h: (b, 0, h)),
            pl.BlockSpec((A_CONV, hin), lambda b, h: (0, h)),
            pl.BlockSpec((1, hin), lambda b, h: (0, h)),
            pl.BlockSpec((1, hin, dqk), lambda b, h: (h, 0, 0)),
            pl.BlockSpec((1, hin, dqk), lambda b, h: (h, 0, 0)),
            pl.BlockSpec((1, hin, dv), lambda b, h: (h, 0, 0)),
            pl.BlockSpec((1, dqk, ng), lambda b, h: (h, 0, 0)),
            pl.BlockSpec((1, dqk, ng), lambda b, h: (h, 0, 0)),
            pl.BlockSpec((1, dv, ng), lambda b, h: (h, 0, 0)),
            pl.BlockSpec((1, ng), lambda b, h: (0, 0)),
        ],
        out_specs=[
            pl.BlockSpec((1, total, hin), lambda b, h: (b, 0, h)),
            pl.BlockSpec((1, total, dqk), lambda b, h: (b, 0, h)),
            pl.BlockSpec((1, total, dqk), lambda b, h: (b, 0, h)),
            pl.BlockSpec((1, total, dv), lambda b, h: (b, 0, h)),
            pl.BlockSpec((1, total, ng), lambda b, h: (b, 0, 0)),
        ],
        out_shape=[
            jax.ShapeDtypeStruct((bsz, total, e), BF16),
            jax.ShapeDtypeStruct((bsz, total, heads * dqk), BF16),
            jax.ShapeDtypeStruct((bsz, total, heads * dqk), BF16),
            jax.ShapeDtypeStruct((bsz, total, heads * dv), BF16),
            jax.ShapeDtypeStruct((bsz, total, ng), F32),
        ],
        compiler_params=_cparams("parallel", "arbitrary"),
        name="mlstm_conv_qkv",
    )(proj, cw, cb, wq, wk, wv, gq, gk, gv, bg)


def _mlstm_chunk(q, k, v, ig, f_pre, c_ref, n_ref, m_ref, scale, emit):
    length = q.shape[0]
    ti = lax.broadcasted_iota(jnp.int32, (length, length), 0)
    si = lax.broadcasted_iota(jnp.int32, (length, length), 1)
    eye = ti == si
    b_col, b = _cumsum_row(_log_sigmoid(f_pre), ti, si)
    m_prev = m_ref[...]
    dmat = jnp.where(si <= ti, b_col - b + ig, NEG_INF)
    m_state = b_col + m_prev
    m_t = jnp.maximum(m_state, jnp.max(dmat, axis=1, keepdims=True))
    s_state = jnp.exp(m_state - m_t)
    yield
    qs = (q.astype(F32) * scale).astype(BF16)
    qk = lax.dot_general(qs, k, (((1,), (1,)), ((), ())), preferred_element_type=F32)
    yield
    w = jnp.exp(dmat - m_t) * qk
    yield
    c_old = c_ref[...]
    n_old = n_ref[...]
    num = s_state * jnp.dot(qs, c_old.astype(BF16), preferred_element_type=F32)
    num = num + jnp.dot(w.astype(BF16), v, preferred_element_type=F32)
    yield
    qn = jnp.sum(qs.astype(F32) * n_old, axis=1, keepdims=True)
    den = s_state * qn + jnp.sum(w, axis=1, keepdims=True)
    emit(num / jnp.maximum(jnp.abs(den), jnp.exp(-m_t)))
    yield
    g_end = b[:, length - 1:length]
    decay = g_end - b + ig
    m_new = jnp.maximum(g_end + m_prev, jnp.max(decay, axis=1, keepdims=True))
    s_old = jnp.exp(g_end + m_prev - m_new)
    w_end = _row_to_col(jnp.exp(decay - m_new), eye)
    kw = k.astype(F32) * w_end
    yield
    c_ref[...] = s_old * c_old + lax.dot_general(
        kw.astype(BF16), v, (((0,), (0,)), ((), ())), preferred_element_type=F32)
    n_ref[...] = s_old * n_old + jnp.sum(kw, axis=0, keepdims=True)
    m_ref[...] = m_new


def _mlstm_kernel(q_ref, k_ref, v_ref, igr_ref, fr_ref, igm_ref, fm_ref, h_ref,
                  c_ref, n_ref, m_ref, *, seq, n_meta, chunk, scale, group, dqk, dv):
    c_ref[...] = jnp.zeros_like(c_ref)
    n_ref[...] = jnp.zeros_like(n_ref)
    m_ref[...] = jnp.zeros_like(m_ref)

    def stages(rows, g, ig, f_pre):
        qc = slice(g * dqk, (g + 1) * dqk)
        vc = slice(g * dv, (g + 1) * dv)

        def emit(h):
            h_ref[0, rows, vc] = h.astype(h_ref.dtype)

        return _mlstm_chunk(q_ref[0, rows, qc], k_ref[0, rows, qc], v_ref[0, rows, vc], ig, f_pre,
                            c_ref.at[g], n_ref.at[g], m_ref.at[g], scale, emit)

    def side_by_side(gens):
        gens = list(gens)
        while gens:
            gens = [gen for gen in gens if next(gen, "done") != "done"]

    side_by_side(stages(pl.ds(seq, n_meta), g, igm_ref[0, g], fm_ref[0, g]) for g in range(group))
    for c in range(seq // chunk):
        side_by_side(stages(pl.ds(c * chunk, chunk), g, igr_ref[0, g, c:c + 1, :], fr_ref[0, g, c:c + 1, :])
                     for g in range(group))


def _mlstm(q, k, v, ig_real, f_real, ig_meta, f_meta, *, seq, n_meta):
    bsz, total, _ = q.shape
    heads = A_HEADS
    dqk = q.shape[2] // heads
    dv = v.shape[2] // heads
    e = heads * dv
    chunk = MLSTM_CHUNK
    nc = seq // chunk
    group = MLSTM_HEAD_GROUP
    kern = functools.partial(_mlstm_kernel, seq=seq, n_meta=n_meta, chunk=chunk, scale=dqk ** -0.5,
                             group=group, dqk=dqk, dv=dv)
    return pl.pallas_call(
        kern,
        grid=(bsz, heads // group),
        in_specs=[
            pl.BlockSpec((1, total, group * dqk), lambda b, h: (b, 0, h)),
            pl.BlockSpec((1, total, group * dqk), lambda b, h: (b, 0, h)),
            pl.BlockSpec((1, total, group * dv), lambda b, h: (b, 0, h)),
            pl.BlockSpec((1, group, nc, chunk), lambda b, h: (b, h, 0, 0)),
            pl.BlockSpec((1, group, nc, chunk), lambda b, h: (b, h, 0, 0)),
            pl.BlockSpec((1, group, 1, n_meta), lambda b, h: (b, h, 0, 0)),
            pl.BlockSpec((1, group, 1, n_meta), lambda b, h: (b, h, 0, 0)),
        ],
        out_specs=pl.BlockSpec((1, total, group * dv), lambda b, h: (b, 0, h)),
        out_shape=jax.ShapeDtypeStruct((bsz, total, e), BF16),
        scratch_shapes=[
            pltpu.VMEM((group, dqk, dv), F32),
            pltpu.VMEM((group, 1, dqk), F32),
            pltpu.VMEM((group, 1, 1), F32),
        ],
        compiler_params=_cparams("parallel", "parallel"),
        name="mlstm_recurrence",
    )(q, k, v, ig_real, f_real, ig_meta, f_meta)


def _split3(x):
    hi = x.astype(BF16).astype(F32)
    r1 = x - hi
    mid = r1.astype(BF16).astype(F32)
    return hi, mid, r1 - mid


def _fold_lanes(x, op):
    out = x[:, :V7X_LANES]
    for t in range(1, x.shape[1] // V7X_LANES):
        out = op(out, x[:, t * V7X_LANES:(t + 1) * V7X_LANES])
    return out


def _cumsum_lanes(x):
    length = x.shape[1]
    lane = lax.broadcasted_iota(jnp.int32, x.shape, 1)
    shift = 1
    while shift < length:
        x = x + jnp.where(lane >= shift, pltpu.roll(x, shift, axis=1), 0.0)
        shift *= 2
    return x


def _fox_gates_kernel(bf3_ref, bf2_ref, fr_ref, fm_ref, pr_ref, pm_ref, *, n_meta, qblock):
    heads, nb = fr_ref.shape[1], fr_ref.shape[2]
    lanes = pr_ref.shape[3]
    rows = heads * nb
    assert nb & (nb - 1) == 0 and heads <= lanes
    c_meta = _cumsum_lanes(_log_sigmoid(fm_ref[0] + bf2_ref[...]) * LOG2E)
    hi_ = lax.broadcasted_iota(jnp.int32, (heads, heads), 0)
    hj_ = lax.broadcasted_iota(jnp.int32, (heads, heads), 1)
    meta_row = _col_to_row(c_meta[:, n_meta - 1:n_meta], hi_ == hj_)
    row_head = lax.shift_right_logical(lax.broadcasted_iota(jnp.int32, (rows, heads), 0), nb.bit_length() - 1)
    col_head = lax.broadcasted_iota(jnp.int32, (rows, heads), 1)
    meta_total = jnp.sum(jnp.where(row_head == col_head, meta_row, 0.0), axis=1, keepdims=True)
    c_blk = _cumsum_lanes(_log_sigmoid(fr_ref[0] + bf3_ref[...]).reshape(rows, qblock) * LOG2E)
    ri = lax.broadcasted_iota(jnp.int32, (rows, rows), 0)
    rj = lax.broadcasted_iota(jnp.int32, (rows, rows), 1)
    tot_row = _col_to_row(c_blk[:, qblock - 1:qblock], ri == rj)
    earlier = (rj < ri) & (rj >= ri - (ri & (nb - 1)))
    offset = jnp.sum(jnp.where(earlier, tot_row, 0.0), axis=1, keepdims=True)
    c_rows = c_blk + (offset + meta_total)
    hi, mid, lo = _split3(c_rows)
    pad = jnp.zeros((lanes - 6 * nb, qblock), F32)
    hi_m, mid_m, lo_m = _split3(jnp.concatenate([c_meta, jnp.zeros((lanes - heads, lanes), F32)], axis=0).T)
    lane_m = lax.broadcasted_iota(jnp.int32, (n_meta, lanes), 1)
    for h in range(heads):
        r = slice(h * nb, (h + 1) * nb)
        pr_ref[0, h] = jnp.concatenate([hi[r], mid[r], lo[r], -hi[r], -mid[r], -lo[r], pad], axis=0).T
        a, b, c = hi_m[:n_meta, h:h + 1], mid_m[:n_meta, h:h + 1], lo_m[:n_meta, h:h + 1]
        pm_ref[0, h] = jnp.where(
            lane_m == 0, a, jnp.where(
                lane_m == nb, b, jnp.where(
                    lane_m == 2 * nb, c, jnp.where(
                        lane_m == 3 * nb, -a, jnp.where(
                            lane_m == 4 * nb, -b, jnp.where(lane_m == 5 * nb, -c, 0.0))))))


def _fox_gates(bf, f_real, f_meta, *, n_meta):
    bsz, heads, nb, qblock = f_real.shape
    assert 6 * nb <= V7X_LANES and f_meta.shape[2] == V7X_LANES
    kern = functools.partial(_fox_gates_kernel, n_meta=n_meta, qblock=qblock)
    return pl.pallas_call(
        kern,
        grid=(bsz,),
        in_specs=[
            pl.BlockSpec((heads, 1, 1), lambda b: (0, 0, 0)),
            pl.BlockSpec((heads, 1), lambda b: (0, 0)),
            pl.BlockSpec((1, heads, nb, qblock), lambda b: (b, 0, 0, 0)),
            pl.BlockSpec((1, heads, V7X_LANES), lambda b: (b, 0, 0)),
        ],
        out_specs=[
            pl.BlockSpec((1, heads, qblock, V7X_LANES), lambda b: (b, 0, 0, 0)),
            pl.BlockSpec((1, heads, n_meta, V7X_LANES), lambda b: (b, 0, 0, 0)),
        ],
        out_shape=[
            jax.ShapeDtypeStruct((bsz, heads, qblock, V7X_LANES), F32),
            jax.ShapeDtypeStruct((bsz, heads, n_meta, V7X_LANES), F32),
        ],
        compiler_params=_cparams("parallel"),
        name="fox_gates",
    )(bf.reshape(heads, 1, 1), bf.reshape(heads, 1), f_real, f_meta)


def _fox_kernel(q_ref, k_ref, v_ref, z_ref, qn_ref, kn_ref, pr_ref, pm_ref, y_ref,
                ka_scr, va_scr, *, seq, n_meta, qblock, scale):
    total = seq + n_meta
    nb = seq // qblock
    hd = V7X_LANES
    group = q_ref.shape[2] // hd
    assert 6 * nb <= hd and qblock % V7X_LANES == 0
    qg = qn_ref[...] * (scale * LOG2E)
    kg = kn_ref[...]
    lane = lax.broadcasted_iota(jnp.int32, (qblock, hd), 1)
    lane_m = lax.broadcasted_iota(jnp.int32, (n_meta, hd), 1)
    ti_m = lax.broadcasted_iota(jnp.int32, (n_meta, n_meta), 0)
    si_m = lax.broadcasted_iota(jnp.int32, (n_meta, n_meta), 1)
    ti = lax.broadcasted_iota(jnp.int32, (qblock, qblock), 0)
    si = lax.broadcasted_iota(jnp.int32, (qblock, qblock), 1)
    causal = si <= ti

    def norm(x, g):
        x = x.astype(F32)
        ms = jnp.mean(x * x, axis=-1, keepdims=True)
        return (x * lax.rsqrt(ms + EPS) * g).astype(BF16)

    gens = [_fox_head(q_ref, k_ref, v_ref, z_ref, y_ref, pr_ref[0, g], pm_ref[0, g], ka_scr.at[g], va_scr.at[g],
                      slice(g * hd, (g + 1) * hd), qg, kg, norm, lane, lane_m, si_m <= ti_m, causal,
                      seq=seq, n_meta=n_meta, qblock=qblock) for g in range(group)]
    while gens:
        gens = [gen for gen in gens if next(gen, "done") != "done"]


def _fox_head(q_ref, k_ref, v_ref, z_ref, y_ref, pieces, pieces_m, ka_scr, va_scr, cols, qg, kg, norm,
              lane, lane_m, causal_m, causal, *, seq, n_meta, qblock):
    total = seq + n_meta
    nb = seq // qblock
    hd = V7X_LANES

    def finish(acc, r0, n_rows):
        rows = pl.ds(r0, n_rows)
        z = z_ref[0, rows, cols].astype(F32)
        y_ref[0, rows, cols] = ((acc[:, :hd] / acc[:, hd:]) * (z * _sigmoid(z))).astype(y_ref.dtype)

    def extra(lane_ids, own, slot, *, query):
        lo_lane, ones_lo = (0, 3 * nb) if query else (3 * nb, 0)
        mine = ((lane_ids == lo_lane + slot) | (lane_ids == lo_lane + nb + slot)
                | (lane_ids == lo_lane + 2 * nb + slot))
        ones = (lane_ids >= ones_lo) & (lane_ids < ones_lo + 3 * nb)
        return jnp.where(mine, own, jnp.where(ones, 1.0, 0.0)).astype(BF16)

    def prepare_keys(i):
        r0 = i * qblock
        ka_scr[r0:r0 + qblock, 0:hd] = norm(k_ref[0, r0:r0 + qblock, cols], kg)
        ka_scr[r0:r0 + qblock, hd:2 * hd] = extra(lane, pieces, i, query=False)
        va_scr[r0:r0 + qblock, 0:hd] = v_ref[0, r0:r0 + qblock, cols]
        va_scr[r0:r0 + qblock, hd:2 * hd] = jnp.ones((qblock, hd), BF16)

    ka_scr[seq:total, 0:hd] = norm(k_ref[0, seq:total, cols], kg)
    ka_scr[seq:total, hd:2 * hd] = extra(lane_m, pieces_m, 0, query=False)
    va_scr[seq:total, 0:hd] = v_ref[0, seq:total, cols]
    va_scr[seq:total, hd:2 * hd] = jnp.ones((n_meta, hd), BF16)

    def logits(qa, r0, n_rows):
        return lax.dot_general(qa, ka_scr[r0:r0 + n_rows, :], (((1,), (1,)), ((), ())),
                               preferred_element_type=F32)

    def weighted(p, r0, n_rows):
        return jnp.dot(p.astype(BF16), va_scr[r0:r0 + n_rows, :], preferred_element_type=F32)

    qa = jnp.concatenate([norm(q_ref[0, seq:total, cols], qg), extra(lane_m, pieces_m, 0, query=True)], axis=1)
    s = jnp.where(causal_m, logits(qa, seq, n_meta), NEG_INF)
    p = jnp.exp2(s - jnp.max(s, axis=1, keepdims=True))
    finish(weighted(p, seq, n_meta), seq, n_meta)
    yield

    for j in range(nb):
        r0 = j * qblock
        prepare_keys(j)
        qa = jnp.concatenate([norm(q_ref[0, r0:r0 + qblock, cols], qg), extra(lane, pieces, j, query=True)], axis=1)
        s_m = logits(qa, seq, n_meta)
        parts = []
        m_part = None
        for i in range(j + 1):
            s = logits(qa, i * qblock, qblock)
            if i == j:
                s = jnp.where(causal, s, NEG_INF)
            parts.append(s)
            folded = _fold_lanes(s, jnp.maximum)
            m_part = folded if m_part is None else jnp.maximum(m_part, folded)
        m_run = jnp.maximum(jnp.max(s_m, axis=1, keepdims=True), jnp.max(m_part, axis=1, keepdims=True))
        acc = weighted(jnp.exp2(s_m - m_run), seq, n_meta)
        for i in range(j + 1):
            acc = acc + weighted(jnp.exp2(parts[i] - m_run), i * qblock, qblock)
        finish(acc, r0, qblock)
        yield


def _fox(proj, qn, kn, pieces, pieces_meta, *, seq, n_meta):
    bsz, total, width = proj.shape
    hd = B_HEAD_DIM
    inner = width // 4
    heads = inner // hd
    qblock = FOX_QBLOCK
    group = FOX_HEAD_GROUP
    gw = group * hd
    nblk = heads // group
    kern = functools.partial(_fox_kernel, seq=seq, n_meta=n_meta, qblock=qblock, scale=hd ** -0.5)
    return pl.pallas_call(
        kern,
        grid=(bsz, nblk),
        in_specs=[
            pl.BlockSpec((1, total, gw), lambda b, h: (b, 0, h)),
            pl.BlockSpec((1, total, gw), lambda b, h: (b, 0, nblk + h)),
            pl.BlockSpec((1, total, gw), lambda b, h: (b, 0, 2 * nblk + h)),
            pl.BlockSpec((1, total, gw), lambda b, h: (b, 0, 3 * nblk + h)),
            pl.BlockSpec((1, hd), lambda b, h: (0, 0)),
            pl.BlockSpec((1, hd), lambda b, h: (0, 0)),
            pl.BlockSpec((1, group, qblock, hd), lambda b, h: (b, h, 0, 0)),
            pl.BlockSpec((1, group, n_meta, hd), lambda b, h: (b, h, 0, 0)),
        ],
        out_specs=pl.BlockSpec((1, total, gw), lambda b, h: (b, 0, h)),
        out_shape=jax.ShapeDtypeStruct((bsz, total, inner), BF16),
        scratch_shapes=[pltpu.VMEM((group, total, 2 * hd), BF16), pltpu.VMEM((group, total, 2 * hd), BF16)],
        compiler_params=_cparams("parallel", "parallel"),
        name="fox_attention",
    )(proj, proj, proj, proj, qn, kn, pieces, pieces_meta)


def _gate_rows(g, seq, n_meta, block):
    bsz, _, heads = g.shape
    gt = jnp.transpose(g, (0, 2, 1))
    real = gt[:, :, :seq].reshape(bsz, heads, seq // block, block)
    meta = gt[:, :, seq:].reshape(bsz, heads, 1, n_meta)
    return real, meta


def kernel(x, meta_tokens, a_norm, a_w_in, a_conv_w, a_conv_b, a_w_q, a_w_k, a_w_v, a_w_gate, a_b_gate,
           a_head_norm, a_skip, a_w_out, b_norm, b_w_in, b_b_f, b_q_norm, b_k_norm, b_w_out, final_norm):
    bsz, seq, d = x.shape
    n_meta = meta_tokens.shape[0]
    total = seq + n_meta
    m = bsz * total
    depth_a = a_norm.shape[0]
    depth_b = b_norm.shape[0]
    assert depth_a == 1 and depth_b == 1, "layer schedule: one mLSTM layer followed by one FoX layer"
    tm = total // 3
    assert total % 3 == 0 and tm % 16 == 0

    meta = jnp.broadcast_to(meta_tokens[None].astype(x.dtype), (bsz, n_meta, d))
    h0 = jnp.concatenate([x, meta], axis=1).reshape(m, d)

    heads = A_HEADS
    e = a_w_out.shape[1]
    dqk = a_w_q.shape[3]
    dv = a_w_v.shape[3]
    nqk = heads * dqk
    w_gate = a_w_gate[0].astype(BF16)
    gq = w_gate[:nqk].reshape(heads, dqk, 2 * heads)
    gk = w_gate[nqk:2 * nqk].reshape(heads, dqk, 2 * heads)
    gv = w_gate[2 * nqk:].reshape(heads, dv, 2 * heads)

    proj = _norm_matmul(h0, a_norm[0][None], a_w_in[0].astype(BF16), tm=2 * tm, tn=1024, name="mlstm_in_proj")
    proj = proj.reshape(bsz, total, 3 * e)
    xc, q, k, v, gates = _conv_qkv(
        proj, a_conv_w[0], a_conv_b[0][None], a_w_q[0].astype(BF16), a_w_k[0].astype(BF16),
        a_w_v[0].astype(BF16), gq, gk, gv, a_b_gate[0][None], seq=seq, n_meta=n_meta)
    ig_real, ig_meta = _gate_rows(gates[..., :heads], seq, n_meta, MLSTM_CHUNK)
    f_real, f_meta = _gate_rows(gates[..., heads:], seq, n_meta, MLSTM_CHUNK)
    hs = _mlstm(q, k, v, ig_real, f_real, ig_meta, f_meta, seq=seq, n_meta=n_meta)
    h1 = _gated_out_proj(hs.reshape(m, e), proj.reshape(m, 3 * e), xc.reshape(m, e), a_head_norm[0][None],
                         a_skip[0][None], a_w_out[0].astype(BF16), h0, tm=_row_tile(m, 512), heads_per_step=4,
                         name="mlstm_out_proj")

    inner = b_w_out.shape[1]
    bheads = inner // B_HEAD_DIM
    w_in = b_w_in[0]
    w_main = jnp.concatenate([w_in[:, :3 * inner].astype(BF16), w_in[:, 3 * inner + bheads:].astype(BF16)], axis=1)
    w_f = jnp.pad(w_in[:, 3 * inner:3 * inner + bheads].astype(BF16), ((0, 0), (0, V7X_LANES - bheads)))
    projb, f_pre = _norm_matmul_extra(h1, b_norm[0][None], w_main, w_f, tm=2 * tm, tn=1024, name="fox_in_proj")
    fr, fm = _gate_rows(f_pre[:, :bheads].reshape(bsz, total, bheads), seq, n_meta, FOX_QBLOCK)
    fm = jnp.pad(fm.reshape(bsz, bheads, n_meta), ((0, 0), (0, 0), (0, V7X_LANES - n_meta)))
    pieces, pieces_meta = _fox_gates(b_b_f[0], fr, fm, n_meta=n_meta)
    yb = _fox(projb.reshape(bsz, total, 4 * inner), b_q_norm[0][None], b_k_norm[0][None],
              pieces, pieces_meta, seq=seq, n_meta=n_meta)
    return _matmul_residual_norm(yb, b_w_out[0].astype(BF16), h1.reshape(bsz, total, d), final_norm[None],
                                 seq=seq, tm=512, name="fox_out_proj_norm")
```

```python
import functools

import jax
import jax.numpy as jnp
from jax import lax
from jax.experimental import pallas as pl
from jax.experimental.pallas import tpu as pltpu

F32 = jnp.float32
BF16 = jnp.bfloat16
EPS = 1e-6
NEG_INF = float("-inf")

V7X_VMEM_BYTES = 64 * 1024 * 1024
V7X_LANES = 128
BF16_SUBLANES = 16
VMEM_LIMIT = 56 * 1024 * 1024

A_HEADS = 8
A_CONV = 4
B_HEAD_DIM = 128
MLSTM_CHUNK = 512
MLSTM_HEAD_GROUP = 2
FOX_QBLOCK = 256
FOX_HEAD_GROUP = 4
CONV_BLOCK = 256


def _row_tile(rows, cap):
    for t in range(cap - cap % BF16_SUBLANES, 0, -BF16_SUBLANES):
        if rows % t == 0:
            return t
    raise ValueError(f"no row tile for {rows} rows")


def _cparams(*sem, flags=None):
    return pltpu.CompilerParams(dimension_semantics=sem, vmem_limit_bytes=VMEM_LIMIT, flags=flags)


LOG2E = 1.4426950408889634


def _sigmoid(x):
    return 1.0 / (1.0 + jnp.exp2(x * -LOG2E))


def _silu(x):
    half = 0.5 * x
    return half * (1.0 + jnp.tanh(half))


def _log_sigmoid(x):
    return jnp.minimum(x, 0.0) - jnp.log(1.0 + jnp.exp(-jnp.abs(x)))


def _row_to_col(row, eye):
    return jnp.sum(jnp.where(eye, row, 0.0), axis=1, keepdims=True)


def _col_to_row(col, eye):
    return jnp.sum(jnp.where(eye, col, 0.0), axis=0, keepdims=True)


def _cumsum_row(row, ti, si):
    col = jnp.sum(jnp.where(si <= ti, row, 0.0), axis=1, keepdims=True)
    return col, _col_to_row(col, ti == si)


def _norm_matmul_kernel(h_ref, g_ref, w_ref, o_ref, xn_ref):
    @pl.when(pl.program_id(1) == 0)
    def _():
        x = h_ref[...]
        ms = jnp.mean(x * x, axis=-1, keepdims=True)
        xn_ref[...] = (x * lax.rsqrt(ms + EPS) * g_ref[...]).astype(xn_ref.dtype)

    o_ref[...] = jnp.dot(xn_ref[...], w_ref[...], preferred_element_type=F32).astype(o_ref.dtype)


def _norm_matmul_prep_kernel(h_ref, g_ref, w_ref, bwin_ref, aout_ref, bout_ref,
                             o_ref, wmain_ref, wf_ref, aout16_ref, bout16_ref, xn_ref, *, n_qkv, n_f):
    @pl.when(pl.program_id(1) == 0)
    def _():
        x = h_ref[...]
        ms = jnp.mean(x * x, axis=-1, keepdims=True)
        xn_ref[...] = (x * lax.rsqrt(ms + EPS) * g_ref[...]).astype(xn_ref.dtype)

    o_ref[...] = jnp.dot(xn_ref[...], w_ref[...], preferred_element_type=F32).astype(o_ref.dtype)

    raw = bwin_ref[...]
    wmain_ref[:, :n_qkv] = raw[:, :n_qkv].astype(BF16)
    wmain_ref[:, n_qkv:] = raw[:, n_qkv + n_f:].astype(BF16)
    lane = lax.broadcasted_iota(jnp.int32, wf_ref.shape, 1)
    wf_ref[...] = jnp.where(lane < n_f, raw[:, n_qkv:n_qkv + V7X_LANES], 0.0).astype(BF16)
    aout16_ref[...] = aout_ref[...].astype(BF16)
    bout16_ref[...] = bout_ref[...].astype(BF16)


def _norm_matmul_prep(h, g, w, b_w_in, a_w_out, b_w_out, *, n_qkv, n_f, tm, tn, name):
    m, d = h.shape
    n = w.shape[1]
    assert m % tm == 0 and n % tn == 0
    ni, nj = m // tm, n // tn

    def slab(arr):
        rows = -(-arr.shape[0] // (ni * nj))
        rows += -rows % BF16_SUBLANES
        assert arr.shape[0] % rows == 0
        last = arr.shape[0] // rows - 1
        return rows, (lambda i, j: (jnp.minimum(i * nj + j, last), 0))

    rb, ib = slab(b_w_in)
    ra, ia = slab(a_w_out)
    ro, io = slab(b_w_out)
    n_main = b_w_in.shape[1] - n_f
    kern = functools.partial(_norm_matmul_prep_kernel, n_qkv=n_qkv, n_f=n_f)
    return pl.pallas_call(
        kern,
        grid=(ni, nj),
        in_specs=[
            pl.BlockSpec((tm, d), lambda i, j: (i, 0)),
            pl.BlockSpec((1, d), lambda i, j: (0, 0)),
            pl.BlockSpec((d, tn), lambda i, j: (0, j)),
            pl.BlockSpec((rb, b_w_in.shape[1]), ib),
            pl.BlockSpec((ra, a_w_out.shape[1]), ia),
            pl.BlockSpec((ro, b_w_out.shape[1]), io),
        ],
        out_specs=[
            pl.BlockSpec((tm, tn), lambda i, j: (i, j)),
            pl.BlockSpec((rb, n_main), ib),
            pl.BlockSpec((rb, V7X_LANES), ib),
            pl.BlockSpec((ra, a_w_out.shape[1]), ia),
            pl.BlockSpec((ro, b_w_out.shape[1]), io),
        ],
        out_shape=[
            jax.ShapeDtypeStruct((m, n), BF16),
            jax.ShapeDtypeStruct((b_w_in.shape[0], n_main), BF16),
            jax.ShapeDtypeStruct((b_w_in.shape[0], V7X_LANES), BF16),
            jax.ShapeDtypeStruct(a_w_out.shape, BF16),
            jax.ShapeDtypeStruct(b_w_out.shape, BF16),
        ],
        scratch_shapes=[pltpu.VMEM((tm, d), BF16)],
        compiler_params=_cparams("arbitrary", "arbitrary"),
        name=name,
    )(h, g, w, b_w_in, a_w_out, b_w_out)


def _norm_matmul_extra_kernel(h_ref, g_ref, w_ref, wx_ref, o_ref, ox_ref, xn_ref):
    @pl.when(pl.program_id(1) == 0)
    def _():
        x = h_ref[...]
        ms = jnp.mean(x * x, axis=-1, keepdims=True)
        xn = (x * lax.rsqrt(ms + EPS) * g_ref[...]).astype(xn_ref.dtype)
        xn_ref[...] = xn
        ox_ref[...] = jnp.dot(xn, wx_ref[...], preferred_element_type=F32)

    o_ref[...] = jnp.dot(xn_ref[...], w_ref[...], preferred_element_type=F32).astype(o_ref.dtype)


def _norm_matmul(h, g, w, *, tm, tn, name):
    m, d = h.shape
    n = w.shape[1]
    assert m % tm == 0 and n % tn == 0
    return pl.pallas_call(
        _norm_matmul_kernel,
        grid=(m // tm, n // tn),
        in_specs=[
            pl.BlockSpec((tm, d), lambda i, j: (i, 0)),
            pl.BlockSpec((1, d), lambda i, j: (0, 0)),
            pl.BlockSpec((d, tn), lambda i, j: (0, j)),
        ],
        out_specs=pl.BlockSpec((tm, tn), lambda i, j: (i, j)),
        out_shape=jax.ShapeDtypeStruct((m, n), BF16),
        scratch_shapes=[pltpu.VMEM((tm, d), BF16)],
        compiler_params=_cparams("parallel", "arbitrary"),
        name=name,
    )(h, g, w)


def _norm_matmul_extra(h, g, w, wx, *, tm, tn, name):
    m, d = h.shape
    n, nx = w.shape[1], wx.shape[1]
    assert m % tm == 0 and n % tn == 0
    return pl.pallas_call(
        _norm_matmul_extra_kernel,
        grid=(m // tm, n // tn),
        in_specs=[
            pl.BlockSpec((tm, d), lambda i, j: (i, 0)),
            pl.BlockSpec((1, d), lambda i, j: (0, 0)),
            pl.BlockSpec((d, tn), lambda i, j: (0, j)),
            pl.BlockSpec((d, nx), lambda i, j: (0, 0)),
        ],
        out_specs=[pl.BlockSpec((tm, tn), lambda i, j: (i, j)), pl.BlockSpec((tm, nx), lambda i, j: (i, 0))],
        out_shape=[jax.ShapeDtypeStruct((m, n), BF16), jax.ShapeDtypeStruct((m, nx), F32)],
        scratch_shapes=[pltpu.VMEM((tm, d), BF16)],
        compiler_params=_cparams("parallel", "arbitrary"),
        name=name,
    )(h, g, w, wx)


def _matmul_residual_kernel(y_ref, w_ref, r_ref, o_ref):
    o_ref[...] = r_ref[...] + jnp.dot(y_ref[...], w_ref[...], preferred_element_type=F32)


def _matmul_residual(y, w, res, *, tm, tn, name):
    m, k = y.shape
    n = w.shape[1]
    assert m % tm == 0 and n % tn == 0
    return pl.pallas_call(
        _matmul_residual_kernel,
        grid=(m // tm, n // tn),
        in_specs=[
            pl.BlockSpec((tm, k), lambda i, j: (i, 0)),
            pl.BlockSpec((k, tn), lambda i, j: (0, j)),
            pl.BlockSpec((tm, tn), lambda i, j: (i, j)),
        ],
        out_specs=pl.BlockSpec((tm, tn), lambda i, j: (i, j)),
        out_shape=jax.ShapeDtypeStruct((m, n), F32),
        compiler_params=_cparams("parallel", "parallel"),
        name=name,
    )(y, w, res)


def _gated_out_proj_kernel(h_ref, o_ref, z_ref, xc_ref, hn_ref, sk_ref, w_ref, r_ref, out_ref, *y_scr,
                           heads_per_step, dv):
    def gate(j):
        cols = slice(j * dv, (j + 1) * dv)
        hg = h_ref[:, cols].astype(F32) * _sigmoid(o_ref[:, cols].astype(F32))
        ms = jnp.mean(hg * hg, axis=-1, keepdims=True)
        hg = hg * lax.rsqrt(ms + EPS) * hn_ref[:, cols]
        z = z_ref[:, cols].astype(F32)
        y = (hg + sk_ref[:, cols] * xc_ref[:, cols].astype(F32)) * _silu(z)
        y_scr[j][...] = y.astype(BF16)

    update = None
    gate(0)
    for j in range(heads_per_step):
        if j + 1 < heads_per_step:
            gate(j + 1)
        part = jnp.dot(y_scr[j][...], w_ref[j * dv:(j + 1) * dv, :], preferred_element_type=F32)
        update = part if update is None else update + part

    @pl.when(pl.program_id(1) == 0)
    def _():
        out_ref[...] = r_ref[...] + update

    @pl.when(pl.program_id(1) != 0)
    def _():
        out_ref[...] += update


def _gated_out_proj(h, proj, xc, hn, sk, w, res, *, tm, heads_per_step, name):
    m, e = h.shape
    d = w.shape[1]
    dv = e // A_HEADS
    kb = heads_per_step * dv
    nk = e // kb
    assert m % tm == 0 and A_HEADS % heads_per_step == 0
    kern = functools.partial(_gated_out_proj_kernel, heads_per_step=heads_per_step, dv=dv)
    return pl.pallas_call(
        kern,
        grid=(m // tm, nk),
        in_specs=[
            pl.BlockSpec((tm, kb), lambda i, k: (i, k)),
            pl.BlockSpec((tm, kb), lambda i, k: (i, nk + k)),
            pl.BlockSpec((tm, kb), lambda i, k: (i, 2 * nk + k)),
            pl.BlockSpec((tm, kb), lambda i, k: (i, k)),
            pl.BlockSpec((1, kb), lambda i, k: (0, k)),
            pl.BlockSpec((1, kb), lambda i, k: (0, k)),
            pl.BlockSpec((kb, d), lambda i, k: (k, 0)),
            pl.BlockSpec((tm, d), lambda i, k: (i, 0)),
        ],
        out_specs=pl.BlockSpec((tm, d), lambda i, k: (i, 0)),
        out_shape=jax.ShapeDtypeStruct((m, d), F32),
        scratch_shapes=[pltpu.VMEM((tm, dv), BF16) for _ in range(heads_per_step)],
        compiler_params=_cparams("parallel", "arbitrary"),
        name=name,
    )(h, proj, proj, xc, hn, sk, w, res)


def _matmul_residual_norm_kernel(y_ref, w_ref, r_ref, g_ref, o_ref):
    x = r_ref[0] + jnp.dot(y_ref[0], w_ref[...], preferred_element_type=F32)
    ms = jnp.mean(x * x, axis=-1, keepdims=True)
    o_ref[0] = x * lax.rsqrt(ms + EPS) * g_ref[...]


def _matmul_residual_norm(y, w, res, g, *, seq, tm, name):
    bsz, _, k = y.shape
    n = w.shape[1]
    assert seq % tm == 0
    return pl.pallas_call(
        _matmul_residual_norm_kernel,
        grid=(bsz, seq // tm),
        in_specs=[
            pl.BlockSpec((1, tm, k), lambda b, i: (b, i, 0)),
            pl.BlockSpec((k, n), lambda b, i: (0, 0)),
            pl.BlockSpec((1, tm, n), lambda b, i: (b, i, 0)),
            pl.BlockSpec((1, n), lambda b, i: (0, 0)),
        ],
        out_specs=pl.BlockSpec((1, tm, n), lambda b, i: (b, i, 0)),
        out_shape=jax.ShapeDtypeStruct((bsz, seq, n), F32),
        compiler_params=_cparams("parallel", "parallel"),
        name=name,
    )(y, w, res, g)


def _conv_qkv_kernel(xm_ref, cw_ref, cb_ref, wq_ref, wk_ref, wv_ref, gq_ref, gk_ref, gv_ref, bg_ref,
                     xc_ref, q_ref, k_ref, v_ref, g_ref, *, seq, n_meta):
    total = seq + n_meta
    blk = CONV_BLOCK
    lead = 8
    half = 16
    cw = cw_ref[...]
    cb = cb_ref[...]

    @pl.when(pl.program_id(1) == 0)
    def _():
        g_ref[0] = jnp.broadcast_to(bg_ref[...], g_ref.shape[1:])

    def project(r0, n_rows, xc):
        rows = pl.ds(r0, n_rows)
        xc_ref[0, rows, :] = xc
        q = jnp.dot(xc, wq_ref[0], preferred_element_type=F32).astype(BF16)
        k = jnp.dot(xc, wk_ref[0], preferred_element_type=F32).astype(BF16)
        v = jnp.dot(xm_ref[0, rows, :], wv_ref[0], preferred_element_type=F32).astype(BF16)
        q_ref[0, rows, :] = q
        k_ref[0, rows, :] = k
        v_ref[0, rows, :] = v
        g_ref[0, rows, :] += (jnp.dot(q, gq_ref[0], preferred_element_type=F32)
                              + jnp.dot(k, gk_ref[0], preferred_element_type=F32)
                              + jnp.dot(v, gv_ref[0], preferred_element_type=F32))

    xm_meta = xm_ref[0, seq:total, :].astype(F32)
    window = jnp.concatenate([jnp.zeros((lead, xm_meta.shape[1]), F32), xm_meta], axis=0)
    acc = cb
    for s in range(A_CONV):
        acc = acc + window[lead - s:lead - s + n_meta] * cw[A_CONV - 1 - s:A_CONV - s]
    project(seq, n_meta, _silu(acc).astype(BF16))

    for i in range(seq // blk):
        r0 = i * blk
        c0 = total - half if i == 0 else r0 - half
        ctx = xm_ref[0, c0:c0 + half, :].astype(F32)[half - lead:]
        window = jnp.concatenate([ctx, xm_ref[0, r0:r0 + blk, :].astype(F32)], axis=0)
        acc = cb
        for s in range(A_CONV):
            acc = acc + window[lead - s:lead - s + blk] * cw[A_CONV - 1 - s:A_CONV - s]
        project(r0, blk, _silu(acc).astype(BF16))


def _conv_qkv(proj, cw, cb, wq, wk, wv, gq, gk, gv, bg, *, seq, n_meta):
    bsz, total, _ = proj.shape
    heads, hin, dqk = wq.shape
    dv = wv.shape[2]
    ng = gq.shape[2]
    e = heads * hin
    kern = functools.partial(_conv_qkv_kernel, seq=seq, n_meta=n_meta)
    return pl.pallas_call(
        kern,
        grid=(bsz, heads),
        in_specs=[
            pl.BlockSpec((1, total, hin), lambda b, h: (b, 0, h)),
            pl.BlockSpec((A_CONV, hin), lambda b, h: (0, h)),
            pl.BlockSpec((1, hin), lambda b, h: (0, h)),
            pl.BlockSpec((1, hin, dqk), lambda b, h: (h, 0, 0)),
            pl.BlockSpec((1, hin, dqk), lambda b, h: (h, 0, 0)),
            pl.BlockSpec((1, hin, dv), lambda b, h: (h, 0, 0)),
            pl.BlockSpec((1, dqk, ng), lambda b, h: (h, 0, 0)),
            pl.BlockSpec((1, dqk, ng), lambda b, h: (h, 0, 0)),
            pl.BlockSpec((1, dv, ng), lambda b, h: (h, 0, 0)),
            pl.BlockSpec((1, ng), lambda b, h: (0, 0)),
        ],
        out_specs=[
            pl.BlockSpec((1, total, hin), lambda b, h: (b, 0, h)),
            pl.BlockSpec((1, total, dqk), lambda b, h: (b, 0, h)),
            pl.BlockSpec((1, total, dqk), lambda b, h: (b, 0, h)),
            pl.BlockSpec((1, total, dv), lambda b, h: (b, 0, h)),
            pl.BlockSpec((1, total, ng), lambda b, h: (b, 0, 0)),
        ],
        out_shape=[
            jax.ShapeDtypeStruct((bsz, total, e), BF16),
            jax.ShapeDtypeStruct((bsz, total, heads * dqk), BF16),
            jax.ShapeDtypeStruct((bsz, total, heads * dqk), BF16),
            jax.ShapeDtypeStruct((bsz, total, heads * dv), BF16),
            jax.ShapeDtypeStruct((bsz, total, ng), F32),
        ],
        compiler_params=_cparams("parallel", "arbitrary"),
        name="mlstm_conv_qkv",
    )(proj, cw, cb, wq, wk, wv, gq, gk, gv, bg)


def _mlstm_chunk(q, k, v, ig, f_pre, c_ref, n_ref, m_ref, scale, emit):
    length = q.shape[0]
    ti = lax.broadcasted_iota(jnp.int32, (length, length), 0)
    si = lax.broadcasted_iota(jnp.int32, (length, length), 1)
    eye = ti == si
    b_col, b = _cumsum_row(_log_sigmoid(f_pre), ti, si)
    m_prev = m_ref[...]
    dmat = jnp.where(si <= ti, b_col - b + ig, NEG_INF)
    m_state = b_col + m_prev
    m_t = jnp.maximum(m_state, jnp.max(dmat, axis=1, keepdims=True))
    s_state = jnp.exp(m_state - m_t)
    yield
    qs = (q.astype(F32) * scale).astype(BF16)
    qk = lax.dot_general(qs, k, (((1,), (1,)), ((), ())), preferred_element_type=F32)
    yield
    w = jnp.exp(dmat - m_t) * qk
    yield
    c_old = c_ref[...]
    n_old = n_ref[...]
    num = s_state * jnp.dot(qs, c_old.astype(BF16), preferred_element_type=F32)
    num = num + jnp.dot(w.astype(BF16), v, preferred_element_type=F32)
    yield
    qn = jnp.sum(qs.astype(F32) * n_old, axis=1, keepdims=True)
    den = s_state * qn + jnp.sum(w, axis=1, keepdims=True)
    emit(num / jnp.maximum(jnp.abs(den), jnp.exp(-m_t)))
    yield
    g_end = b[:, length - 1:length]
    decay = g_end - b + ig
    m_new = jnp.maximum(g_end + m_prev, jnp.max(decay, axis=1, keepdims=True))
    s_old = jnp.exp(g_end + m_prev - m_new)
    w_end = _row_to_col(jnp.exp(decay - m_new), eye)
    kw = k.astype(F32) * w_end
    yield
    c_ref[...] = s_old * c_old + lax.dot_general(
        kw.astype(BF16), v, (((0,), (0,)), ((), ())), preferred_element_type=F32)
    n_ref[...] = s_old * n_old + jnp.sum(kw, axis=0, keepdims=True)
    m_ref[...] = m_new


def _mlstm_kernel(q_ref, k_ref, v_ref, igr_ref, fr_ref, igm_ref, fm_ref, h_ref,
                  c_ref, n_ref, m_ref, *, seq, n_meta, chunk, scale, group, dqk, dv):
    c_ref[...] = jnp.zeros_like(c_ref)
    n_ref[...] = jnp.zeros_like(n_ref)
    m_ref[...] = jnp.zeros_like(m_ref)

    def stages(rows, g, ig, f_pre):
        qc = slice(g * dqk, (g + 1) * dqk)
        vc = slice(g * dv, (g + 1) * dv)

        def emit(h):
            h_ref[0, rows, vc] = h.astype(h_ref.dtype)

        return _mlstm_chunk(q_ref[0, rows, qc], k_ref[0, rows, qc], v_ref[0, rows, vc], ig, f_pre,
                            c_ref.at[g], n_ref.at[g], m_ref.at[g], scale, emit)

    def side_by_side(gens):
        gens = list(gens)
        while gens:
            gens = [gen for gen in gens if next(gen, "done") != "done"]

    side_by_side(stages(pl.ds(seq, n_meta), g, igm_ref[0, g], fm_ref[0, g]) for g in range(group))
    for c in range(seq // chunk):
        side_by_side(stages(pl.ds(c * chunk, chunk), g, igr_ref[0, g, c:c + 1, :], fr_ref[0, g, c:c + 1, :])
                     for g in range(group))


def _mlstm(q, k, v, ig_real, f_real, ig_meta, f_meta, *, seq, n_meta):
    bsz, total, _ = q.shape
    heads = A_HEADS
    dqk = q.shape[2] // heads
    dv = v.shape[2] // heads
    e = heads * dv
    chunk = MLSTM_CHUNK
    nc = seq // chunk
    group = MLSTM_HEAD_GROUP
    kern = functools.partial(_mlstm_kernel, seq=seq, n_meta=n_meta, chunk=chunk, scale=dqk ** -0.5,
                             group=group, dqk=dqk, dv=dv)
    return pl.pallas_call(
        kern,
        grid=(bsz, heads // group),
        in_specs=[
            pl.BlockSpec((1, total, group * dqk), lambda b, h: (b, 0, h)),
            pl.BlockSpec((1, total, group * dqk), lambda b, h: (b, 0, h)),
            pl.BlockSpec((1, total, group * dv), lambda b, h: (b, 0, h)),
            pl.BlockSpec((1, group, nc, chunk), lambda b, h: (b, h, 0, 0)),
            pl.BlockSpec((1, group, nc, chunk), lambda b, h: (b, h, 0, 0)),
            pl.BlockSpec((1, group, 1, n_meta), lambda b, h: (b, h, 0, 0)),
            pl.BlockSpec((1, group, 1, n_meta), lambda b, h: (b, h, 0, 0)),
        ],
        out_specs=pl.BlockSpec((1, total, group * dv), lambda b, h: (b, 0, h)),
        out_shape=jax.ShapeDtypeStruct((bsz, total, e), BF16),
        scratch_shapes=[
            pltpu.VMEM((group, dqk, dv), F32),
            pltpu.VMEM((group, 1, dqk), F32),
            pltpu.VMEM((group, 1, 1), F32),
        ],
        compiler_params=_cparams("parallel", "parallel"),
        name="mlstm_recurrence",
    )(q, k, v, ig_real, f_real, ig_meta, f_meta)


def _split3(x):
    hi = x.astype(BF16).astype(F32)
    r1 = x - hi
    mid = r1.astype(BF16).astype(F32)
    return hi, mid, r1 - mid


def _fold_lanes(x, op):
    out = x[:, :V7X_LANES]
    for t in range(1, x.shape[1] // V7X_LANES):
        out = op(out, x[:, t * V7X_LANES:(t + 1) * V7X_LANES])
    return out


def _cumsum_lanes(x):
    length = x.shape[1]
    lane = lax.broadcasted_iota(jnp.int32, x.shape, 1)
    shift = 1
    while shift < length:
        x = x + jnp.where(lane >= shift, pltpu.roll(x, shift, axis=1), 0.0)
        shift *= 2
    return x


def _fox_gates_kernel(bf3_ref, bf2_ref, fr_ref, fm_ref, pr_ref, pm_ref, *, n_meta, qblock):
    heads, nb = fr_ref.shape[1], fr_ref.shape[2]
    lanes = pr_ref.shape[3]
    rows = heads * nb
    assert nb & (nb - 1) == 0 and heads <= lanes
    c_meta = _cumsum_lanes(_log_sigmoid(fm_ref[0] + bf2_ref[...]) * LOG2E)
    hi_ = lax.broadcasted_iota(jnp.int32, (heads, heads), 0)
    hj_ = lax.broadcasted_iota(jnp.int32, (heads, heads), 1)
    meta_row = _col_to_row(c_meta[:, n_meta - 1:n_meta], hi_ == hj_)
    row_head = lax.shift_right_logical(lax.broadcasted_iota(jnp.int32, (rows, heads), 0), nb.bit_length() - 1)
    col_head = lax.broadcasted_iota(jnp.int32, (rows, heads), 1)
    meta_total = jnp.sum(jnp.where(row_head == col_head, meta_row, 0.0), axis=1, keepdims=True)
    c_blk = _cumsum_lanes(_log_sigmoid(fr_ref[0] + bf3_ref[...]).reshape(rows, qblock) * LOG2E)
    ri = lax.broadcasted_iota(jnp.int32, (rows, rows), 0)
    rj = lax.broadcasted_iota(jnp.int32, (rows, rows), 1)
    tot_row = _col_to_row(c_blk[:, qblock - 1:qblock], ri == rj)
    earlier = (rj < ri) & (rj >= ri - (ri & (nb - 1)))
    offset = jnp.sum(jnp.where(earlier, tot_row, 0.0), axis=1, keepdims=True)
    c_rows = c_blk + (offset + meta_total)
    hi, mid, lo = _split3(c_rows)
    pad = jnp.zeros((lanes - 6 * nb, qblock), F32)
    hi_m, mid_m, lo_m = _split3(jnp.concatenate([c_meta, jnp.zeros((lanes - heads, lanes), F32)], axis=0).T)
    lane_m = lax.broadcasted_iota(jnp.int32, (n_meta, lanes), 1)
    for h in range(heads):
        r = slice(h * nb, (h + 1) * nb)
        pr_ref[0, h] = jnp.concatenate([hi[r], mid[r], lo[r], -hi[r], -mid[r], -lo[r], pad], axis=0).T
        a, b, c = hi_m[:n_meta, h:h + 1], mid_m[:n_meta, h:h + 1], lo_m[:n_meta, h:h + 1]
        pm_ref[0, h] = jnp.where(
            lane_m == 0, a, jnp.where(
                lane_m == nb, b, jnp.where(
                    lane_m == 2 * nb, c, jnp.where(
                        lane_m == 3 * nb, -a, jnp.where(
                            lane_m == 4 * nb, -b, jnp.where(lane_m == 5 * nb, -c, 0.0))))))


def _fox_gates(bf, f_real, f_meta, *, n_meta):
    bsz, heads, nb, qblock = f_real.shape
    assert 6 * nb <= V7X_LANES and f_meta.shape[2] == V7X_LANES
    kern = functools.partial(_fox_gates_kernel, n_meta=n_meta, qblock=qblock)
    return pl.pallas_call(
        kern,
        grid=(bsz,),
        in_specs=[
            pl.BlockSpec((heads, 1, 1), lambda b: (0, 0, 0)),
            pl.BlockSpec((heads, 1), lambda b: (0, 0)),
            pl.BlockSpec((1, heads, nb, qblock), lambda b: (b, 0, 0, 0)),
            pl.BlockSpec((1, heads, V7X_LANES), lambda b: (b, 0, 0)),
        ],
        out_specs=[
            pl.BlockSpec((1, heads, qblock, V7X_LANES), lambda b: (b, 0, 0, 0)),
            pl.BlockSpec((1, heads, n_meta, V7X_LANES), lambda b: (b, 0, 0, 0)),
        ],
        out_shape=[
            jax.ShapeDtypeStruct((bsz, heads, qblock, V7X_LANES), F32),
            jax.ShapeDtypeStruct((bsz, heads, n_meta, V7X_LANES), F32),
        ],
        compiler_params=_cparams("parallel"),
        name="fox_gates",
    )(bf.reshape(heads, 1, 1), bf.reshape(heads, 1), f_real, f_meta)


def _fox_kernel(q_ref, k_ref, v_ref, z_ref, qn_ref, kn_ref, pr_ref, pm_ref, y_ref,
                ka_scr, va_scr, *, seq, n_meta, qblock, scale):
    total = seq + n_meta
    nb = seq // qblock
    hd = V7X_LANES
    group = q_ref.shape[2] // hd
    assert 6 * nb <= hd and qblock % V7X_LANES == 0
    qg = qn_ref[...] * (scale * LOG2E)
    kg = kn_ref[...]
    lane = lax.broadcasted_iota(jnp.int32, (qblock, hd), 1)
    lane_m = lax.broadcasted_iota(jnp.int32, (n_meta, hd), 1)
    ti_m = lax.broadcasted_iota(jnp.int32, (n_meta, n_meta), 0)
    si_m = lax.broadcasted_iota(jnp.int32, (n_meta, n_meta), 1)
    ti = lax.broadcasted_iota(jnp.int32, (qblock, qblock), 0)
    si = lax.broadcasted_iota(jnp.int32, (qblock, qblock), 1)
    causal = si <= ti

    def norm(x, g):
        x = x.astype(F32)
        ms = jnp.mean(x * x, axis=-1, keepdims=True)
        return (x * lax.rsqrt(ms + EPS) * g).astype(BF16)

    gens = [_fox_head(q_ref, k_ref, v_ref, z_ref, y_ref, pr_ref[0, g], pm_ref[0, g], ka_scr.at[g], va_scr.at[g],
                      slice(g * hd, (g + 1) * hd), qg, kg, norm, lane, lane_m, si_m <= ti_m, causal,
                      seq=seq, n_meta=n_meta, qblock=qblock) for g in range(group)]
    while gens:
        gens = [gen for gen in gens if next(gen, "done") != "done"]


def _fox_head(q_ref, k_ref, v_ref, z_ref, y_ref, pieces, pieces_m, ka_scr, va_scr, cols, qg, kg, norm,
              lane, lane_m, causal_m, causal, *, seq, n_meta, qblock):
    total = seq + n_meta
    nb = seq // qblock
    hd = V7X_LANES

    def finish(acc, r0, n_rows):
        rows = pl.ds(r0, n_rows)
        z = z_ref[0, rows, cols].astype(F32)
        y_ref[0, rows, cols] = ((acc[:, :hd] / acc[:, hd:]) * _silu(z)).astype(y_ref.dtype)

    def extra(lane_ids, own, slot, *, query):
        lo_lane, ones_lo = (0, 3 * nb) if query else (3 * nb, 0)
        mine = ((lane_ids == lo_lane + slot) | (lane_ids == lo_lane + nb + slot)
                | (lane_ids == lo_lane + 2 * nb + slot))
        ones = (lane_ids >= ones_lo) & (lane_ids < ones_lo + 3 * nb)
        return jnp.where(mine, own, jnp.where(ones, 1.0, 0.0)).astype(BF16)

    def prepare_keys(i):
        r0 = i * qblock
        ka_scr[r0:r0 + qblock, 0:hd] = norm(k_ref[0, r0:r0 + qblock, cols], kg)
        ka_scr[r0:r0 + qblock, hd:2 * hd] = extra(lane, pieces, i, query=False)
        va_scr[r0:r0 + qblock, 0:hd] = v_ref[0, r0:r0 + qblock, cols]
        va_scr[r0:r0 + qblock, hd:2 * hd] = jnp.ones((qblock, hd), BF16)

    ka_scr[seq:total, 0:hd] = norm(k_ref[0, seq:total, cols], kg)
    ka_scr[seq:total, hd:2 * hd] = extra(lane_m, pieces_m, 0, query=False)
    va_scr[seq:total, 0:hd] = v_ref[0, seq:total, cols]
    va_scr[seq:total, hd:2 * hd] = jnp.ones((n_meta, hd), BF16)

    def logits(qa, r0, n_rows):
        return lax.dot_general(qa, ka_scr[r0:r0 + n_rows, :], (((1,), (1,)), ((), ())),
                               preferred_element_type=F32)

    def weighted(p, r0, n_rows):
        return jnp.dot(p.astype(BF16), va_scr[r0:r0 + n_rows, :], preferred_element_type=F32)

    qa = jnp.concatenate([norm(q_ref[0, seq:total, cols], qg), extra(lane_m, pieces_m, 0, query=True)], axis=1)
    s = jnp.where(causal_m, logits(qa, seq, n_meta), NEG_INF)
    p = jnp.exp2(s - jnp.max(s, axis=1, keepdims=True))
    finish(weighted(p, seq, n_meta), seq, n_meta)
    yield

    for j in range(nb):
        r0 = j * qblock
        prepare_keys(j)
        qa = jnp.concatenate([norm(q_ref[0, r0:r0 + qblock, cols], qg), extra(lane, pieces, j, query=True)], axis=1)
        s_m = logits(qa, seq, n_meta)
        parts = []
        m_part = None
        for i in range(j + 1):
            s = logits(qa, i * qblock, qblock)
            if i == j:
                s = jnp.where(causal, s, NEG_INF)
            parts.append(s)
            folded = _fold_lanes(s, jnp.maximum)
            m_part = folded if m_part is None else jnp.maximum(m_part, folded)
        m_run = jnp.maximum(jnp.max(s_m, axis=1, keepdims=True), jnp.max(m_part, axis=1, keepdims=True))
        acc = weighted(jnp.exp2(s_m - m_run), seq, n_meta)
        for i in range(j + 1):
            acc = acc + weighted(jnp.exp2(parts[i] - m_run), i * qblock, qblock)
        finish(acc, r0, qblock)
        yield


def _fox(proj, qn, kn, pieces, pieces_meta, *, seq, n_meta):
    bsz, total, width = proj.shape
    hd = B_HEAD_DIM
    inner = width // 4
    heads = inner // hd
    qblock = FOX_QBLOCK
    group = FOX_HEAD_GROUP
    gw = group * hd
    nblk = heads // group
    kern = functools.partial(_fox_kernel, seq=seq, n_meta=n_meta, qblock=qblock, scale=hd ** -0.5)
    return pl.pallas_call(
        kern,
        grid=(bsz, nblk),
        in_specs=[
            pl.BlockSpec((1, total, gw), lambda b, h: (b, 0, h)),
            pl.BlockSpec((1, total, gw), lambda b, h: (b, 0, nblk + h)),
            pl.BlockSpec((1, total, gw), lambda b, h: (b, 0, 2 * nblk + h)),
            pl.BlockSpec((1, total, gw), lambda b, h: (b, 0, 3 * nblk + h)),
            pl.BlockSpec((1, hd), lambda b, h: (0, 0)),
            pl.BlockSpec((1, hd), lambda b, h: (0, 0)),
            pl.BlockSpec((1, group, qblock, hd), lambda b, h: (b, h, 0, 0)),
            pl.BlockSpec((1, group, n_meta, hd), lambda b, h: (b, h, 0, 0)),
        ],
        out_specs=pl.BlockSpec((1, total, gw), lambda b, h: (b, 0, h)),
        out_shape=jax.ShapeDtypeStruct((bsz, total, inner), BF16),
        scratch_shapes=[pltpu.VMEM((group, total, 2 * hd), BF16), pltpu.VMEM((group, total, 2 * hd), BF16)],
        compiler_params=_cparams("parallel", "parallel"),
        name="fox_attention",
    )(proj, proj, proj, proj, qn, kn, pieces, pieces_meta)


def _gate_rows(g, seq, n_meta, block):
    bsz, _, heads = g.shape
    gt = jnp.transpose(g, (0, 2, 1))
    real = gt[:, :, :seq].reshape(bsz, heads, seq // block, block)
    meta = gt[:, :, seq:].reshape(bsz, heads, 1, n_meta)
    return real, meta


def kernel(x, meta_tokens, a_norm, a_w_in, a_conv_w, a_conv_b, a_w_q, a_w_k, a_w_v, a_w_gate, a_b_gate,
           a_head_norm, a_skip, a_w_out, b_norm, b_w_in, b_b_f, b_q_norm, b_k_norm, b_w_out, final_norm):
    bsz, seq, d = x.shape
    n_meta = meta_tokens.shape[0]
    total = seq + n_meta
    m = bsz * total
    depth_a = a_norm.shape[0]
    depth_b = b_norm.shape[0]
    assert depth_a == 1 and depth_b == 1, "layer schedule: one mLSTM layer followed by one FoX layer"
    tm = total // 3
    assert total % 3 == 0 and tm % 16 == 0

    meta = jnp.broadcast_to(meta_tokens[None].astype(x.dtype), (bsz, n_meta, d))
    h0 = jnp.concatenate([x, meta], axis=1).reshape(m, d)

    heads = A_HEADS
    e = a_w_out.shape[1]
    dqk = a_w_q.shape[3]
    dv = a_w_v.shape[3]
    nqk = heads * dqk
    w_gate = a_w_gate[0].astype(BF16)
    gq = w_gate[:nqk].reshape(heads, dqk, 2 * heads)
    gk = w_gate[nqk:2 * nqk].reshape(heads, dqk, 2 * heads)
    gv = w_gate[2 * nqk:].reshape(heads, dv, 2 * heads)

    inner = b_w_out.shape[1]
    bheads = inner // B_HEAD_DIM
    proj, w_main, w_f, a_w_out16, b_w_out16 = _norm_matmul_prep(
        h0, a_norm[0][None], a_w_in[0].astype(BF16), b_w_in[0], a_w_out[0], b_w_out[0],
        n_qkv=3 * inner, n_f=bheads, tm=2 * tm, tn=1024, name="mlstm_in_proj")
    proj = proj.reshape(bsz, total, 3 * e)
    xc, q, k, v, gates = _conv_qkv(
        proj, a_conv_w[0], a_conv_b[0][None], a_w_q[0].astype(BF16), a_w_k[0].astype(BF16),
        a_w_v[0].astype(BF16), gq, gk, gv, a_b_gate[0][None], seq=seq, n_meta=n_meta)
    ig_real, ig_meta = _gate_rows(gates[..., :heads], seq, n_meta, MLSTM_CHUNK)
    f_real, f_meta = _gate_rows(gates[..., heads:], seq, n_meta, MLSTM_CHUNK)
    hs = _mlstm(q, k, v, ig_real, f_real, ig_meta, f_meta, seq=seq, n_meta=n_meta)
    h1 = _gated_out_proj(hs.reshape(m, e), proj.reshape(m, 3 * e), xc.reshape(m, e), a_head_norm[0][None],
                         a_skip[0][None], a_w_out16, h0, tm=_row_tile(m, 512), heads_per_step=4,
                         name="mlstm_out_proj")

    projb, f_pre = _norm_matmul_extra(h1, b_norm[0][None], w_main, w_f, tm=2 * tm, tn=1024, name="fox_in_proj")
    fr, fm = _gate_rows(f_pre[:, :bheads].reshape(bsz, total, bheads), seq, n_meta, FOX_QBLOCK)
    fm = jnp.pad(fm.reshape(bsz, bheads, n_meta), ((0, 0), (0, 0), (0, V7X_LANES - n_meta)))
    pieces, pieces_meta = _fox_gates(b_b_f[0], fr, fm, n_meta=n_meta)
    yb = _fox(projb.reshape(bsz, total, 4 * inner), b_q_norm[0][None], b_k_norm[0][None],
              pieces, pieces_meta, seq=seq, n_meta=n_meta)
    return _matmul_residual_norm(yb, b_w_out16, h1.reshape(bsz, total, d), final_norm[None],
                                 seq=seq, tm=512, name="fox_out_proj_norm")
```

```python
import functools

import jax
import jax.numpy as jnp
from jax import lax
from jax.experimental import pallas as pl
from jax.experimental.pallas import tpu as pltpu

F32 = jnp.float32
BF16 = jnp.bfloat16
EPS = 1e-6
NEG_INF = float("-inf")

V7X_VMEM_BYTES = 64 * 1024 * 1024
V7X_LANES = 128
BF16_SUBLANES = 16
VMEM_LIMIT = 56 * 1024 * 1024

A_HEADS = 8
A_CONV = 4
B_HEAD_DIM = 128
MLSTM_CHUNK = 512
MLSTM_HEAD_GROUP = 2
FOX_QBLOCK = 256
FOX_HEAD_GROUP = 4
CONV_BLOCK = 256


def _row_tile(rows, cap):
    for t in range(cap - cap % BF16_SUBLANES, 0, -BF16_SUBLANES):
        if rows % t == 0:
            return t
    raise ValueError(f"no row tile for {rows} rows")


def _cparams(*sem, flags=None):
    return pltpu.CompilerParams(dimension_semantics=sem, vmem_limit_bytes=VMEM_LIMIT, flags=flags)


LOG2E = 1.4426950408889634


def _sigmoid(x):
    return 1.0 / (1.0 + jnp.exp2(x * -LOG2E))


def _silu(x):
    half = 0.5 * x
    return half * (1.0 + jnp.tanh(half))


def _log_sigmoid(x):
    return jnp.minimum(x, 0.0) - jnp.log(1.0 + jnp.exp(-jnp.abs(x)))


def _row_to_col(row, eye):
    return jnp.sum(jnp.where(eye, row, 0.0), axis=1, keepdims=True)


def _col_to_row(col, eye):
    return jnp.sum(jnp.where(eye, col, 0.0), axis=0, keepdims=True)


def _cumsum_row(row, ti, si):
    col = jnp.sum(jnp.where(si <= ti, row, 0.0), axis=1, keepdims=True)
    return col, _col_to_row(col, ti == si)


def _norm_matmul_kernel(h_ref, g_ref, w_ref, o_ref, xn_ref):
    @pl.when(pl.program_id(1) == 0)
    def _():
        x = h_ref[...]
        ms = jnp.mean(x * x, axis=-1, keepdims=True)
        xn_ref[...] = (x * lax.rsqrt(ms + EPS) * g_ref[...]).astype(xn_ref.dtype)

    o_ref[...] = jnp.dot(xn_ref[...], w_ref[...], preferred_element_type=F32).astype(o_ref.dtype)


def _norm_matmul_prep_kernel(h_ref, g_ref, w_ref, src_a_ref, src_b_ref, src_f_ref, aout_ref, bout_ref,
                             o_ref, wmain_ref, wf_ref, aout16_ref, bout16_ref, xn_ref, *, qkv_slabs, n_f, nj, last):
    @pl.when(pl.program_id(1) == 0)
    def _():
        x = h_ref[...]
        ms = jnp.mean(x * x, axis=-1, keepdims=True)
        xn_ref[...] = (x * lax.rsqrt(ms + EPS) * g_ref[...]).astype(xn_ref.dtype)

    o_ref[...] = jnp.dot(xn_ref[...], w_ref[...], preferred_element_type=F32).astype(o_ref.dtype)

    slab = jnp.minimum(pl.program_id(0) * nj + pl.program_id(1), last)
    rows = src_a_ref.shape[0]
    shifted = jnp.concatenate([src_a_ref[n_f:rows, :], src_b_ref[0:n_f, :]], axis=0)
    wmain_ref[...] = jnp.where(slab >= qkv_slabs, shifted, src_a_ref[...]).astype(BF16)
    wf_ref[0:n_f, :] = src_f_ref[...].astype(BF16)
    wf_ref[n_f:, :] = jnp.zeros((wf_ref.shape[0] - n_f, wf_ref.shape[1]), BF16)
    aout16_ref[...] = aout_ref[...].astype(BF16)
    bout16_ref[...] = bout_ref[...].astype(BF16)


def _norm_matmul_prep(h, g, w, b_w_in_t, a_w_out, b_w_out, *, n_qkv, n_f, tm, tn, name):
    m, d = h.shape
    n = w.shape[1]
    assert m % tm == 0 and n % tn == 0
    ni, nj = m // tm, n // tn
    steps = ni * nj

    def slab_rows(total_rows, also_divides=None):
        for rows in range(BF16_SUBLANES, total_rows + 1, BF16_SUBLANES):
            if total_rows % rows == 0 and total_rows // rows <= steps and (also_divides is None or also_divides % rows == 0):
                return rows
        raise ValueError("no slab size")

    def index(last):
        return lambda i, j: (jnp.minimum(i * nj + j, last), 0)

    n_main = b_w_in_t.shape[0] - n_f
    assert n_f % 8 == 0 and n_qkv % n_f == 0
    rb = slab_rows(n_main, n_qkv)
    last_b = n_main // rb - 1
    ra = slab_rows(a_w_out.shape[0])
    ro = slab_rows(b_w_out.shape[0])
    kern = functools.partial(_norm_matmul_prep_kernel, qkv_slabs=n_qkv // rb, n_f=n_f, nj=nj, last=last_b)
    return pl.pallas_call(
        kern,
        grid=(ni, nj),
        in_specs=[
            pl.BlockSpec((tm, d), lambda i, j: (i, 0)),
            pl.BlockSpec((1, d), lambda i, j: (0, 0)),
            pl.BlockSpec((d, tn), lambda i, j: (0, j)),
            pl.BlockSpec((rb, d), index(last_b)),
            pl.BlockSpec((rb, d), lambda i, j: (jnp.minimum(i * nj + j, last_b) + 1, 0)),
            pl.BlockSpec((n_f, d), lambda i, j: (n_qkv // n_f, 0)),
            pl.BlockSpec((ra, a_w_out.shape[1]), index(a_w_out.shape[0] // ra - 1)),
            pl.BlockSpec((ro, b_w_out.shape[1]), index(b_w_out.shape[0] // ro - 1)),
        ],
        out_specs=[
            pl.BlockSpec((tm, tn), lambda i, j: (i, j)),
            pl.BlockSpec((rb, d), index(last_b)),
            pl.BlockSpec((V7X_LANES, d), lambda i, j: (0, 0)),
            pl.BlockSpec((ra, a_w_out.shape[1]), index(a_w_out.shape[0] // ra - 1)),
            pl.BlockSpec((ro, b_w_out.shape[1]), index(b_w_out.shape[0] // ro - 1)),
        ],
        out_shape=[
            jax.ShapeDtypeStruct((m, n), BF16),
            jax.ShapeDtypeStruct((n_main, d), BF16),
            jax.ShapeDtypeStruct((V7X_LANES, d), BF16),
            jax.ShapeDtypeStruct(a_w_out.shape, BF16),
            jax.ShapeDtypeStruct(b_w_out.shape, BF16),
        ],
        scratch_shapes=[pltpu.VMEM((tm, d), BF16)],
        compiler_params=_cparams("arbitrary", "arbitrary"),
        name=name,
    )(h, g, w, b_w_in_t, b_w_in_t, b_w_in_t, a_w_out, b_w_out)


def _dot_nt(a, b_t):
    return lax.dot_general(a, b_t, (((1,), (1,)), ((), ())), preferred_element_type=F32)


def _norm_matmul_extra_kernel(h_ref, g_ref, wt_ref, wxt_ref, o_ref, ox_ref, xn_ref):
    @pl.when(pl.program_id(1) == 0)
    def _():
        x = h_ref[...]
        ms = jnp.mean(x * x, axis=-1, keepdims=True)
        xn = (x * lax.rsqrt(ms + EPS) * g_ref[...]).astype(xn_ref.dtype)
        xn_ref[...] = xn
        ox_ref[...] = _dot_nt(xn, wxt_ref[...])

    o_ref[...] = _dot_nt(xn_ref[...], wt_ref[...]).astype(o_ref.dtype)


def _norm_matmul(h, g, w, *, tm, tn, name):
    m, d = h.shape
    n = w.shape[1]
    assert m % tm == 0 and n % tn == 0
    return pl.pallas_call(
        _norm_matmul_kernel,
        grid=(m // tm, n // tn),
        in_specs=[
            pl.BlockSpec((tm, d), lambda i, j: (i, 0)),
            pl.BlockSpec((1, d), lambda i, j: (0, 0)),
            pl.BlockSpec((d, tn), lambda i, j: (0, j)),
        ],
        out_specs=pl.BlockSpec((tm, tn), lambda i, j: (i, j)),
        out_shape=jax.ShapeDtypeStruct((m, n), BF16),
        scratch_shapes=[pltpu.VMEM((tm, d), BF16)],
        compiler_params=_cparams("parallel", "arbitrary"),
        name=name,
    )(h, g, w)


def _norm_matmul_extra(h, g, w_t, wx_t, *, tm, tn, name):
    m, d = h.shape
    n, nx = w_t.shape[0], wx_t.shape[0]
    assert m % tm == 0 and n % tn == 0
    return pl.pallas_call(
        _norm_matmul_extra_kernel,
        grid=(m // tm, n // tn),
        in_specs=[
            pl.BlockSpec((tm, d), lambda i, j: (i, 0)),
            pl.BlockSpec((1, d), lambda i, j: (0, 0)),
            pl.BlockSpec((tn, d), lambda i, j: (j, 0)),
            pl.BlockSpec((nx, d), lambda i, j: (0, 0)),
        ],
        out_specs=[pl.BlockSpec((tm, tn), lambda i, j: (i, j)), pl.BlockSpec((tm, nx), lambda i, j: (i, 0))],
        out_shape=[jax.ShapeDtypeStruct((m, n), BF16), jax.ShapeDtypeStruct((m, nx), F32)],
        scratch_shapes=[pltpu.VMEM((tm, d), BF16)],
        compiler_params=_cparams("parallel", "arbitrary"),
        name=name,
    )(h, g, w_t, wx_t)


def _matmul_residual_kernel(y_ref, w_ref, r_ref, o_ref):
    o_ref[...] = r_ref[...] + jnp.dot(y_ref[...], w_ref[...], preferred_element_type=F32)


def _matmul_residual(y, w, res, *, tm, tn, name):
    m, k = y.shape
    n = w.shape[1]
    assert m % tm == 0 and n % tn == 0
    return pl.pallas_call(
        _matmul_residual_kernel,
        grid=(m // tm, n // tn),
        in_specs=[
            pl.BlockSpec((tm, k), lambda i, j: (i, 0)),
            pl.BlockSpec((k, tn), lambda i, j: (0, j)),
            pl.BlockSpec((tm, tn), lambda i, j: (i, j)),
        ],
        out_specs=pl.BlockSpec((tm, tn), lambda i, j: (i, j)),
        out_shape=jax.ShapeDtypeStruct((m, n), F32),
        compiler_params=_cparams("parallel", "parallel"),
        name=name,
    )(y, w, res)


def _gated_out_proj_kernel(h_ref, o_ref, z_ref, xc_ref, hn_ref, sk_ref, w_ref, r_ref, out_ref, *y_scr,
                           heads_per_step, dv):
    def gate(j):
        cols = slice(j * dv, (j + 1) * dv)
        hg = h_ref[:, cols].astype(F32) * _sigmoid(o_ref[:, cols].astype(F32))
        ms = jnp.mean(hg * hg, axis=-1, keepdims=True)
        hg = hg * lax.rsqrt(ms + EPS) * hn_ref[:, cols]
        z = z_ref[:, cols].astype(F32)
        y = (hg + sk_ref[:, cols] * xc_ref[:, cols].astype(F32)) * _silu(z)
        y_scr[j][...] = y.astype(BF16)

    update = None
    gate(0)
    for j in range(heads_per_step):
        if j + 1 < heads_per_step:
            gate(j + 1)
        part = jnp.dot(y_scr[j][...], w_ref[j * dv:(j + 1) * dv, :], preferred_element_type=F32)
        update = part if update is None else update + part

    @pl.when(pl.program_id(1) == 0)
    def _():
        out_ref[...] = r_ref[...] + update

    @pl.when(pl.program_id(1) != 0)
    def _():
        out_ref[...] += update


def _gated_out_proj(h, proj, xc, hn, sk, w, res, *, tm, heads_per_step, name):
    m, e = h.shape
    d = w.shape[1]
    dv = e // A_HEADS
    kb = heads_per_step * dv
    nk = e // kb
    assert m % tm == 0 and A_HEADS % heads_per_step == 0
    kern = functools.partial(_gated_out_proj_kernel, heads_per_step=heads_per_step, dv=dv)
    return pl.pallas_call(
        kern,
        grid=(m // tm, nk),
        in_specs=[
            pl.BlockSpec((tm, kb), lambda i, k: (i, k)),
            pl.BlockSpec((tm, kb), lambda i, k: (i, nk + k)),
            pl.BlockSpec((tm, kb), lambda i, k: (i, 2 * nk + k)),
            pl.BlockSpec((tm, kb), lambda i, k: (i, k)),
            pl.BlockSpec((1, kb), lambda i, k: (0, k)),
            pl.BlockSpec((1, kb), lambda i, k: (0, k)),
            pl.BlockSpec((kb, d), lambda i, k: (k, 0)),
            pl.BlockSpec((tm, d), lambda i, k: (i, 0)),
        ],
        out_specs=pl.BlockSpec((tm, d), lambda i, k: (i, 0)),
        out_shape=jax.ShapeDtypeStruct((m, d), F32),
        scratch_shapes=[pltpu.VMEM((tm, dv), BF16) for _ in range(heads_per_step)],
        compiler_params=_cparams("parallel", "arbitrary"),
        name=name,
    )(h, proj, proj, xc, hn, sk, w, res)


def _matmul_residual_norm_kernel(y_ref, w_ref, r_ref, g_ref, o_ref):
    x = r_ref[0] + jnp.dot(y_ref[0], w_ref[...], preferred_element_type=F32)
    ms = jnp.mean(x * x, axis=-1, keepdims=True)
    o_ref[0] = x * lax.rsqrt(ms + EPS) * g_ref[...]


def _matmul_residual_norm(y, w, res, g, *, seq, tm, name):
    bsz, _, k = y.shape
    n = w.shape[1]
    assert seq % tm == 0
    return pl.pallas_call(
        _matmul_residual_norm_kernel,
        grid=(bsz, seq // tm),
        in_specs=[
            pl.BlockSpec((1, tm, k), lambda b, i: (b, i, 0)),
            pl.BlockSpec((k, n), lambda b, i: (0, 0)),
            pl.BlockSpec((1, tm, n), lambda b, i: (b, i, 0)),
            pl.BlockSpec((1, n), lambda b, i: (0, 0)),
        ],
        out_specs=pl.BlockSpec((1, tm, n), lambda b, i: (b, i, 0)),
        out_shape=jax.ShapeDtypeStruct((bsz, seq, n), F32),
        compiler_params=_cparams("parallel", "parallel"),
        name=name,
    )(y, w, res, g)


def _conv_qkv_kernel(xm_ref, cw_ref, cb_ref, wq_ref, wk_ref, wv_ref, gq_ref, gk_ref, gv_ref, bg_ref,
                     xc_ref, q_ref, k_ref, v_ref, g_ref, *, seq, n_meta):
    total = seq + n_meta
    blk = CONV_BLOCK
    lead = 8
    half = 16
    cw = cw_ref[...]
    cb = cb_ref[...]

    @pl.when(pl.program_id(1) == 0)
    def _():
        g_ref[0] = jnp.broadcast_to(bg_ref[...], g_ref.shape[1:])

    def project(r0, n_rows, xc):
        rows = pl.ds(r0, n_rows)
        xc_ref[0, rows, :] = xc
        q = jnp.dot(xc, wq_ref[0], preferred_element_type=F32).astype(BF16)
        k = jnp.dot(xc, wk_ref[0], preferred_element_type=F32).astype(BF16)
        v = jnp.dot(xm_ref[0, rows, :], wv_ref[0], preferred_element_type=F32).astype(BF16)
        q_ref[0, rows, :] = q
        k_ref[0, rows, :] = k
        v_ref[0, rows, :] = v
        g_ref[0, rows, :] += (jnp.dot(q, gq_ref[0], preferred_element_type=F32)
                              + jnp.dot(k, gk_ref[0], preferred_element_type=F32)
                              + jnp.dot(v, gv_ref[0], preferred_element_type=F32))

    xm_meta = xm_ref[0, seq:total, :].astype(F32)
    window = jnp.concatenate([jnp.zeros((lead, xm_meta.shape[1]), F32), xm_meta], axis=0)
    acc = cb
    for s in range(A_CONV):
        acc = acc + window[lead - s:lead - s + n_meta] * cw[A_CONV - 1 - s:A_CONV - s]
    project(seq, n_meta, _silu(acc).astype(BF16))

    for i in range(seq // blk):
        r0 = i * blk
        c0 = total - half if i == 0 else r0 - half
        ctx = xm_ref[0, c0:c0 + half, :].astype(F32)[half - lead:]
        window = jnp.concatenate([ctx, xm_ref[0, r0:r0 + blk, :].astype(F32)], axis=0)
        acc = cb
        for s in range(A_CONV):
            acc = acc + window[lead - s:lead - s + blk] * cw[A_CONV - 1 - s:A_CONV - s]
        project(r0, blk, _silu(acc).astype(BF16))


def _conv_qkv(proj, cw, cb, wq, wk, wv, gq, gk, gv, bg, *, seq, n_meta):
    bsz, total, _ = proj.shape
    heads, hin, dqk = wq.shape
    dv = wv.shape[2]
    ng = gq.shape[2]
    e = heads * hin
    kern = functools.partial(_conv_qkv_kernel, seq=seq, n_meta=n_meta)
    return pl.pallas_call(
        kern,
        grid=(bsz, heads),
        in_specs=[
            pl.BlockSpec((1, total, hin), lambda b, h: (b, 0, h)),
            pl.BlockSpec((A_CONV, hin), lambda b, h: (0, h)),
            pl.BlockSpec((1, hin), lambda b, h: (0, h)),
            pl.BlockSpec((1, hin, dqk), lambda b, h: (h, 0, 0)),
            pl.BlockSpec((1, hin, dqk), lambda b, h: (h, 0, 0)),
            pl.BlockSpec((1, hin, dv), lambda b, h: (h, 0, 0)),
            pl.BlockSpec((1, dqk, ng), lambda b, h: (h, 0, 0)),
            pl.BlockSpec((1, dqk, ng), lambda b, h: (h, 0, 0)),
            pl.BlockSpec((1, dv, ng), lambda b, h: (h, 0, 0)),
            pl.BlockSpec((1, ng), lambda b, h: (0, 0)),
        ],
        out_specs=[
            pl.BlockSpec((1, total, hin), lambda b, h: (b, 0, h)),
            pl.BlockSpec((1, total, dqk), lambda b, h: (b, 0, h)),
            pl.BlockSpec((1, total, dqk), lambda b, h: (b, 0, h)),
            pl.BlockSpec((1, total, dv), lambda b, h: (b, 0, h)),
            pl.BlockSpec((1, total, ng), lambda b, h: (b, 0, 0)),
        ],
        out_shape=[
            jax.ShapeDtypeStruct((bsz, total, e), BF16),
            jax.ShapeDtypeStruct((bsz, total, heads * dqk), BF16),
            jax.ShapeDtypeStruct((bsz, total, heads * dqk), BF16),
            jax.ShapeDtypeStruct((bsz, total, heads * dv), BF16),
            jax.ShapeDtypeStruct((bsz, total, ng), F32),
        ],
        compiler_params=_cparams("parallel", "arbitrary"),
        name="mlstm_conv_qkv",
    )(proj, cw, cb, wq, wk, wv, gq, gk, gv, bg)


def _mlstm_chunk(q, k, v, ig, f_pre, c_ref, n_ref, m_ref, scale, emit):
    length = q.shape[0]
    ti = lax.broadcasted_iota(jnp.int32, (length, length), 0)
    si = lax.broadcasted_iota(jnp.int32, (length, length), 1)
    eye = ti == si
    b_col, b = _cumsum_row(_log_sigmoid(f_pre), ti, si)
    m_prev = m_ref[...]
    dmat = jnp.where(si <= ti, b_col - b + ig, NEG_INF)
    m_state = b_col + m_prev
    m_t = jnp.maximum(m_state, jnp.max(dmat, axis=1, keepdims=True))
    s_state = jnp.exp(m_state - m_t)
    yield
    qs = (q.astype(F32) * scale).astype(BF16)
    qk = lax.dot_general(qs, k, (((1,), (1,)), ((), ())), preferred_element_type=F32)
    yield
    w = jnp.exp(dmat - m_t) * qk
    yield
    c_old = c_ref[...]
    n_old = n_ref[...]
    num = s_state * jnp.dot(qs, c_old.astype(BF16), preferred_element_type=F32)
    num = num + jnp.dot(w.astype(BF16), v, preferred_element_type=F32)
    yield
    qn = jnp.sum(qs.astype(F32) * n_old, axis=1, keepdims=True)
    den = s_state * qn + jnp.sum(w, axis=1, keepdims=True)
    emit(num / jnp.maximum(jnp.abs(den), jnp.exp(-m_t)))
    yield
    g_end = b[:, length - 1:length]
    decay = g_end - b + ig
    m_new = jnp.maximum(g_end + m_prev, jnp.max(decay, axis=1, keepdims=True))
    s_old = jnp.exp(g_end + m_prev - m_new)
    w_end = _row_to_col(jnp.exp(decay - m_new), eye)
    kw = k.astype(F32) * w_end
    yield
    c_ref[...] = s_old * c_old + lax.dot_general(
        kw.astype(BF16), v, (((0,), (0,)), ((), ())), preferred_element_type=F32)
    n_ref[...] = s_old * n_old + jnp.sum(kw, axis=0, keepdims=True)
    m_ref[...] = m_new


def _mlstm_kernel(q_ref, k_ref, v_ref, igr_ref, fr_ref, igm_ref, fm_ref, h_ref,
                  c_ref, n_ref, m_ref, *, seq, n_meta, chunk, scale, group, dqk, dv):
    c_ref[...] = jnp.zeros_like(c_ref)
    n_ref[...] = jnp.zeros_like(n_ref)
    m_ref[...] = jnp.zeros_like(m_ref)

    def stages(rows, g, ig, f_pre):
        qc = slice(g * dqk, (g + 1) * dqk)
        vc = slice(g * dv, (g + 1) * dv)

        def emit(h):
            h_ref[0, rows, vc] = h.astype(h_ref.dtype)

        return _mlstm_chunk(q_ref[0, rows, qc], k_ref[0, rows, qc], v_ref[0, rows, vc], ig, f_pre,
                            c_ref.at[g], n_ref.at[g], m_ref.at[g], scale, emit)

    def side_by_side(gens):
        gens = list(gens)
        while gens:
            gens = [gen for gen in gens if next(gen, "done") != "done"]

    side_by_side(stages(pl.ds(seq, n_meta), g, igm_ref[0, g], fm_ref[0, g]) for g in range(group))
    for c in range(seq // chunk):
        side_by_side(stages(pl.ds(c * chunk, chunk), g, igr_ref[0, g, c:c + 1, :], fr_ref[0, g, c:c + 1, :])
                     for g in range(group))


def _mlstm(q, k, v, ig_real, f_real, ig_meta, f_meta, *, seq, n_meta):
    bsz, total, _ = q.shape
    heads = A_HEADS
    dqk = q.shape[2] // heads
    dv = v.shape[2] // heads
    e = heads * dv
    chunk = MLSTM_CHUNK
    nc = seq // chunk
    group = MLSTM_HEAD_GROUP
    kern = functools.partial(_mlstm_kernel, seq=seq, n_meta=n_meta, chunk=chunk, scale=dqk ** -0.5,
                             group=group, dqk=dqk, dv=dv)
    return pl.pallas_call(
        kern,
        grid=(bsz, heads // group),
        in_specs=[
            pl.BlockSpec((1, total, group * dqk), lambda b, h: (b, 0, h)),
            pl.BlockSpec((1, total, group * dqk), lambda b, h: (b, 0, h)),
            pl.BlockSpec((1, total, group * dv), lambda b, h: (b, 0, h)),
            pl.BlockSpec((1, group, nc, chunk), lambda b, h: (b, h, 0, 0)),
            pl.BlockSpec((1, group, nc, chunk), lambda b, h: (b, h, 0, 0)),
            pl.BlockSpec((1, group, 1, n_meta), lambda b, h: (b, h, 0, 0)),
            pl.BlockSpec((1, group, 1, n_meta), lambda b, h: (b, h, 0, 0)),
        ],
        out_specs=pl.BlockSpec((1, total, group * dv), lambda b, h: (b, 0, h)),
        out_shape=jax.ShapeDtypeStruct((bsz, total, e), BF16),
        scratch_shapes=[
            pltpu.VMEM((group, dqk, dv), F32),
            pltpu.VMEM((group, 1, dqk), F32),
            pltpu.VMEM((group, 1, 1), F32),
        ],
        compiler_params=_cparams("parallel", "parallel"),
        name="mlstm_recurrence",
    )(q, k, v, ig_real, f_real, ig_meta, f_meta)


def _split3(x):
    hi = x.astype(BF16).astype(F32)
    r1 = x - hi
    mid = r1.astype(BF16).astype(F32)
    return hi, mid, r1 - mid


def _fold_lanes(x, op):
    out = x[:, :V7X_LANES]
    for t in range(1, x.shape[1] // V7X_LANES):
        out = op(out, x[:, t * V7X_LANES:(t + 1) * V7X_LANES])
    return out


def _cumsum_lanes(x):
    length = x.shape[1]
    lane = lax.broadcasted_iota(jnp.int32, x.shape, 1)
    shift = 1
    while shift < length:
        x = x + jnp.where(lane >= shift, pltpu.roll(x, shift, axis=1), 0.0)
        shift *= 2
    return x


def _fox_gates_kernel(bf3_ref, bf2_ref, fr_ref, fm_ref, pr_ref, pm_ref, *, n_meta, qblock):
    heads, nb = fr_ref.shape[1], fr_ref.shape[2]
    lanes = pr_ref.shape[3]
    rows = heads * nb
    assert nb & (nb - 1) == 0 and heads <= lanes
    c_meta = _cumsum_lanes(_log_sigmoid(fm_ref[0] + bf2_ref[...]) * LOG2E)
    hi_ = lax.broadcasted_iota(jnp.int32, (heads, heads), 0)
    hj_ = lax.broadcasted_iota(jnp.int32, (heads, heads), 1)
    meta_row = _col_to_row(c_meta[:, n_meta - 1:n_meta], hi_ == hj_)
    row_head = lax.shift_right_logical(lax.broadcasted_iota(jnp.int32, (rows, heads), 0), nb.bit_length() - 1)
    col_head = lax.broadcasted_iota(jnp.int32, (rows, heads), 1)
    meta_total = jnp.sum(jnp.where(row_head == col_head, meta_row, 0.0), axis=1, keepdims=True)
    c_blk = _cumsum_lanes(_log_sigmoid(fr_ref[0] + bf3_ref[...]).reshape(rows, qblock) * LOG2E)
    ri = lax.broadcasted_iota(jnp.int32, (rows, rows), 0)
    rj = lax.broadcasted_iota(jnp.int32, (rows, rows), 1)
    tot_row = _col_to_row(c_blk[:, qblock - 1:qblock], ri == rj)
    earlier = (rj < ri) & (rj >= ri - (ri & (nb - 1)))
    offset = jnp.sum(jnp.where(earlier, tot_row, 0.0), axis=1, keepdims=True)
    c_rows = c_blk + (offset + meta_total)
    hi, mid, lo = _split3(c_rows)
    pad = jnp.zeros((lanes - 6 * nb, qblock), F32)
    hi_m, mid_m, lo_m = _split3(jnp.concatenate([c_meta, jnp.zeros((lanes - heads, lanes), F32)], axis=0).T)
    lane_m = lax.broadcasted_iota(jnp.int32, (n_meta, lanes), 1)
    for h in range(heads):
        r = slice(h * nb, (h + 1) * nb)
        pr_ref[0, h] = jnp.concatenate([hi[r], mid[r], lo[r], -hi[r], -mid[r], -lo[r], pad], axis=0).T
        a, b, c = hi_m[:n_meta, h:h + 1], mid_m[:n_meta, h:h + 1], lo_m[:n_meta, h:h + 1]
        pm_ref[0, h] = jnp.where(
            lane_m == 0, a, jnp.where(
                lane_m == nb, b, jnp.where(
                    lane_m == 2 * nb, c, jnp.where(
                        lane_m == 3 * nb, -a, jnp.where(
                            lane_m == 4 * nb, -b, jnp.where(lane_m == 5 * nb, -c, 0.0))))))


def _fox_gates(bf, f_real, f_meta, *, n_meta):
    bsz, heads, nb, qblock = f_real.shape
    assert 6 * nb <= V7X_LANES and f_meta.shape[2] == V7X_LANES
    kern = functools.partial(_fox_gates_kernel, n_meta=n_meta, qblock=qblock)
    return pl.pallas_call(
        kern,
        grid=(bsz,),
        in_specs=[
            pl.BlockSpec((heads, 1, 1), lambda b: (0, 0, 0)),
            pl.BlockSpec((heads, 1), lambda b: (0, 0)),
            pl.BlockSpec((1, heads, nb, qblock), lambda b: (b, 0, 0, 0)),
            pl.BlockSpec((1, heads, V7X_LANES), lambda b: (b, 0, 0)),
        ],
        out_specs=[
            pl.BlockSpec((1, heads, qblock, V7X_LANES), lambda b: (b, 0, 0, 0)),
            pl.BlockSpec((1, heads, n_meta, V7X_LANES), lambda b: (b, 0, 0, 0)),
        ],
        out_shape=[
            jax.ShapeDtypeStruct((bsz, heads, qblock, V7X_LANES), F32),
            jax.ShapeDtypeStruct((bsz, heads, n_meta, V7X_LANES), F32),
        ],
        compiler_params=_cparams("parallel"),
        name="fox_gates",
    )(bf.reshape(heads, 1, 1), bf.reshape(heads, 1), f_real, f_meta)


def _fox_kernel(q_ref, k_ref, v_ref, z_ref, qn_ref, kn_ref, pr_ref, pm_ref, y_ref,
                ka_scr, va_scr, *, seq, n_meta, qblock, scale):
    total = seq + n_meta
    nb = seq // qblock
    hd = V7X_LANES
    group = q_ref.shape[2] // hd
    assert 6 * nb <= hd and qblock % V7X_LANES == 0
    qg = qn_ref[...] * (scale * LOG2E)
    kg = kn_ref[...]
    lane = lax.broadcasted_iota(jnp.int32, (qblock, hd), 1)
    lane_m = lax.broadcasted_iota(jnp.int32, (n_meta, hd), 1)
    ti_m = lax.broadcasted_iota(jnp.int32, (n_meta, n_meta), 0)
    si_m = lax.broadcasted_iota(jnp.int32, (n_meta, n_meta), 1)
    ti = lax.broadcasted_iota(jnp.int32, (qblock, qblock), 0)
    si = lax.broadcasted_iota(jnp.int32, (qblock, qblock), 1)
    causal = si <= ti

    def norm(x, g):
        x = x.astype(F32)
        ms = jnp.mean(x * x, axis=-1, keepdims=True)
        return (x * lax.rsqrt(ms + EPS) * g).astype(BF16)

    gens = [_fox_head(q_ref, k_ref, v_ref, z_ref, y_ref, pr_ref[0, g], pm_ref[0, g], ka_scr.at[g], va_scr.at[g],
                      slice(g * hd, (g + 1) * hd), qg, kg, norm, lane, lane_m, si_m <= ti_m, causal,
                      seq=seq, n_meta=n_meta, qblock=qblock) for g in range(group)]
    while gens:
        gens = [gen for gen in gens if next(gen, "done") != "done"]


def _fox_head(q_ref, k_ref, v_ref, z_ref, y_ref, pieces, pieces_m, ka_scr, va_scr, cols, qg, kg, norm,
              lane, lane_m, causal_m, causal, *, seq, n_meta, qblock):
    total = seq + n_meta
    nb = seq // qblock
    hd = V7X_LANES

    def finish(acc, r0, n_rows):
        rows = pl.ds(r0, n_rows)
        z = z_ref[0, rows, cols].astype(F32)
        y_ref[0, rows, cols] = ((acc[:, :hd] / acc[:, hd:]) * _silu(z)).astype(y_ref.dtype)

    def extra(lane_ids, own, slot, *, query):
        lo_lane, ones_lo = (0, 3 * nb) if query else (3 * nb, 0)
        mine = ((lane_ids == lo_lane + slot) | (lane_ids == lo_lane + nb + slot)
                | (lane_ids == lo_lane + 2 * nb + slot))
        ones = (lane_ids >= ones_lo) & (lane_ids < ones_lo + 3 * nb)
        return jnp.where(mine, own, jnp.where(ones, 1.0, 0.0)).astype(BF16)

    def prepare_keys(i):
        r0 = i * qblock
        ka_scr[r0:r0 + qblock, 0:hd] = norm(k_ref[0, r0:r0 + qblock, cols], kg)
        ka_scr[r0:r0 + qblock, hd:2 * hd] = extra(lane, pieces, i, query=False)
        va_scr[r0:r0 + qblock, 0:hd] = v_ref[0, r0:r0 + qblock, cols]
        va_scr[r0:r0 + qblock, hd:2 * hd] = jnp.ones((qblock, hd), BF16)

    ka_scr[seq:total, 0:hd] = norm(k_ref[0, seq:total, cols], kg)
    ka_scr[seq:total, hd:2 * hd] = extra(lane_m, pieces_m, 0, query=False)
    va_scr[seq:total, 0:hd] = v_ref[0, seq:total, cols]
    va_scr[seq:total, hd:2 * hd] = jnp.ones((n_meta, hd), BF16)

    def logits(qa, r0, n_rows):
        return lax.dot_general(qa, ka_scr[r0:r0 + n_rows, :], (((1,), (1,)), ((), ())),
                               preferred_element_type=F32)

    def weighted(p, r0, n_rows):
        return jnp.dot(p.astype(BF16), va_scr[r0:r0 + n_rows, :], preferred_element_type=F32)

    qa = jnp.concatenate([norm(q_ref[0, seq:total, cols], qg), extra(lane_m, pieces_m, 0, query=True)], axis=1)
    s = jnp.where(causal_m, logits(qa, seq, n_meta), NEG_INF)
    p = jnp.exp2(s - jnp.max(s, axis=1, keepdims=True))
    finish(weighted(p, seq, n_meta), seq, n_meta)
    yield

    for j in range(nb):
        r0 = j * qblock
        prepare_keys(j)
        qa = jnp.concatenate([norm(q_ref[0, r0:r0 + qblock, cols], qg), extra(lane, pieces, j, query=True)], axis=1)
        s_m = logits(qa, seq, n_meta)
        parts = []
        m_part = None
        for i in range(j + 1):
            s = logits(qa, i * qblock, qblock)
            if i == j:
                s = jnp.where(causal, s, NEG_INF)
            parts.append(s)
            folded = _fold_lanes(s, jnp.maximum)
            m_part = folded if m_part is None else jnp.maximum(m_part, folded)
        m_run = jnp.maximum(jnp.max(s_m, axis=1, keepdims=True), jnp.max(m_part, axis=1, keepdims=True))
        acc = weighted(jnp.exp2(s_m - m_run), seq, n_meta)
        for i in range(j + 1):
            acc = acc + weighted(jnp.exp2(parts[i] - m_run), i * qblock, qblock)
        finish(acc, r0, qblock)
        yield


def _fox(proj, qn, kn, pieces, pieces_meta, *, seq, n_meta):
    bsz, total, width = proj.shape
    hd = B_HEAD_DIM
    inner = width // 4
    heads = inner // hd
    qblock = FOX_QBLOCK
    group = FOX_HEAD_GROUP
    gw = group * hd
    nblk = heads // group
    kern = functools.partial(_fox_kernel, seq=seq, n_meta=n_meta, qblock=qblock, scale=hd ** -0.5)
    return pl.pallas_call(
        kern,
        grid=(bsz, nblk),
        in_specs=[
            pl.BlockSpec((1, total, gw), lambda b, h: (b, 0, h)),
            pl.BlockSpec((1, total, gw), lambda b, h: (b, 0, nblk + h)),
            pl.BlockSpec((1, total, gw), lambda b, h: (b, 0, 2 * nblk + h)),
            pl.BlockSpec((1, total, gw), lambda b, h: (b, 0, 3 * nblk + h)),
            pl.BlockSpec((1, hd), lambda b, h: (0, 0)),
            pl.BlockSpec((1, hd), lambda b, h: (0, 0)),
            pl.BlockSpec((1, group, qblock, hd), lambda b, h: (b, h, 0, 0)),
            pl.BlockSpec((1, group, n_meta, hd), lambda b, h: (b, h, 0, 0)),
        ],
        out_specs=pl.BlockSpec((1, total, gw), lambda b, h: (b, 0, h)),
        out_shape=jax.ShapeDtypeStruct((bsz, total, inner), BF16),
        scratch_shapes=[pltpu.VMEM((group, total, 2 * hd), BF16), pltpu.VMEM((group, total, 2 * hd), BF16)],
        compiler_params=_cparams("parallel", "parallel"),
        name="fox_attention",
    )(proj, proj, proj, proj, qn, kn, pieces, pieces_meta)


def _gate_rows(g, seq, n_meta, block):
    bsz, _, heads = g.shape
    gt = jnp.transpose(g, (0, 2, 1))
    real = gt[:, :, :seq].reshape(bsz, heads, seq // block, block)
    meta = gt[:, :, seq:].reshape(bsz, heads, 1, n_meta)
    return real, meta


def kernel(x, meta_tokens, a_norm, a_w_in, a_conv_w, a_conv_b, a_w_q, a_w_k, a_w_v, a_w_gate, a_b_gate,
           a_head_norm, a_skip, a_w_out, b_norm, b_w_in, b_b_f, b_q_norm, b_k_norm, b_w_out, final_norm):
    bsz, seq, d = x.shape
    n_meta = meta_tokens.shape[0]
    total = seq + n_meta
    m = bsz * total
    depth_a = a_norm.shape[0]
    depth_b = b_norm.shape[0]
    assert depth_a == 1 and depth_b == 1, "layer schedule: one mLSTM layer followed by one FoX layer"
    tm = total // 3
    assert total % 3 == 0 and tm % 16 == 0

    meta = jnp.broadcast_to(meta_tokens[None].astype(x.dtype), (bsz, n_meta, d))
    h0 = jnp.concatenate([x, meta], axis=1).reshape(m, d)

    heads = A_HEADS
    e = a_w_out.shape[1]
    dqk = a_w_q.shape[3]
    dv = a_w_v.shape[3]
    nqk = heads * dqk
    w_gate = a_w_gate[0].astype(BF16)
    gq = w_gate[:nqk].reshape(heads, dqk, 2 * heads)
    gk = w_gate[nqk:2 * nqk].reshape(heads, dqk, 2 * heads)
    gv = w_gate[2 * nqk:].reshape(heads, dv, 2 * heads)

    inner = b_w_out.shape[1]
    bheads = inner // B_HEAD_DIM
    proj, w_main_t, w_f_t, a_w_out16, b_w_out16 = _norm_matmul_prep(
        h0, a_norm[0][None], a_w_in[0].astype(BF16), b_w_in[0].T, a_w_out[0], b_w_out[0],
        n_qkv=3 * inner, n_f=bheads, tm=2 * tm, tn=1024, name="mlstm_in_proj")
    proj = proj.reshape(bsz, total, 3 * e)
    xc, q, k, v, gates = _conv_qkv(
        proj, a_conv_w[0], a_conv_b[0][None], a_w_q[0].astype(BF16), a_w_k[0].astype(BF16),
        a_w_v[0].astype(BF16), gq, gk, gv, a_b_gate[0][None], seq=seq, n_meta=n_meta)
    ig_real, ig_meta = _gate_rows(gates[..., :heads], seq, n_meta, MLSTM_CHUNK)
    f_real, f_meta = _gate_rows(gates[..., heads:], seq, n_meta, MLSTM_CHUNK)
    hs = _mlstm(q, k, v, ig_real, f_real, ig_meta, f_meta, seq=seq, n_meta=n_meta)
    h1 = _gated_out_proj(hs.reshape(m, e), proj.reshape(m, 3 * e), xc.reshape(m, e), a_head_norm[0][None],
                         a_skip[0][None], a_w_out16, h0, tm=_row_tile(m, 512), heads_per_step=4,
                         name="mlstm_out_proj")

    projb, f_pre = _norm_matmul_extra(h1, b_norm[0][None], w_main_t, w_f_t, tm=2 * tm, tn=1024, name="fox_in_proj")
    fr, fm = _gate_rows(f_pre[:, :bheads].reshape(bsz, total, bheads), seq, n_meta, FOX_QBLOCK)
    fm = jnp.pad(fm.reshape(bsz, bheads, n_meta), ((0, 0), (0, 0), (0, V7X_LANES - n_meta)))
    pieces, pieces_meta = _fox_gates(b_b_f[0], fr, fm, n_meta=n_meta)
    yb = _fox(projb.reshape(bsz, total, 4 * inner), b_q_norm[0][None], b_k_norm[0][None],
              pieces, pieces_meta, seq=seq, n_meta=n_meta)
    return _matmul_residual_norm(yb, b_w_out16, h1.reshape(bsz, total, d), final_norm[None],
                                 seq=seq, tm=512, name="fox_out_proj_norm")
```

```python
import functools

import jax
import jax.numpy as jnp
from jax import lax
from jax.experimental import pallas as pl
from jax.experimental.pallas import tpu as pltpu

F32 = jnp.float32
BF16 = jnp.bfloat16
EPS = 1e-6
NEG_INF = float("-inf")

V7X_VMEM_BYTES = 64 * 1024 * 1024
V7X_LANES = 128
BF16_SUBLANES = 16
VMEM_LIMIT = 56 * 1024 * 1024

A_HEADS = 8
A_CONV = 4
B_HEAD_DIM = 128
MLSTM_CHUNK = 512
MLSTM_HEAD_GROUP = 2
FOX_QBLOCK = 256
FOX_HEAD_GROUP = 4
CONV_BLOCK = 256
CONV_HEAD_GROUP = 2


def _row_tile(rows, cap):
    for t in range(cap - cap % BF16_SUBLANES, 0, -BF16_SUBLANES):
        if rows % t == 0:
            return t
    raise ValueError(f"no row tile for {rows} rows")


def _cparams(*sem, flags=None):
    return pltpu.CompilerParams(dimension_semantics=sem, vmem_limit_bytes=VMEM_LIMIT, flags=flags)


LOG2E = 1.4426950408889634


def _sigmoid(x):
    return 1.0 / (1.0 + jnp.exp2(x * -LOG2E))


def _silu(x):
    half = 0.5 * x
    return half * (1.0 + jnp.tanh(half))


def _log_sigmoid(x):
    return jnp.minimum(x, 0.0) - jnp.log(1.0 + jnp.exp(-jnp.abs(x)))


def _row_to_col(row, eye):
    return jnp.sum(jnp.where(eye, row, 0.0), axis=1, keepdims=True)


def _col_to_row(col, eye):
    return jnp.sum(jnp.where(eye, col, 0.0), axis=0, keepdims=True)


def _cumsum_row(row, ti, si):
    col = jnp.sum(jnp.where(si <= ti, row, 0.0), axis=1, keepdims=True)
    return col, _col_to_row(col, ti == si)


def _norm_matmul_kernel(h_ref, g_ref, w_ref, o_ref, xn_ref):
    @pl.when(pl.program_id(1) == 0)
    def _():
        x = h_ref[...]
        ms = jnp.mean(x * x, axis=-1, keepdims=True)
        xn_ref[...] = (x * lax.rsqrt(ms + EPS) * g_ref[...]).astype(xn_ref.dtype)

    o_ref[...] = jnp.dot(xn_ref[...], w_ref[...], preferred_element_type=F32).astype(o_ref.dtype)


def _norm_matmul_prep_kernel(h_ref, g_ref, w_ref, src_a_ref, src_b_ref, src_f_ref, aout_ref, bout_ref,
                             o_ref, wmain_ref, wf_ref, aout16_ref, bout16_ref, xn_ref, *, qkv_slabs, n_f, nj, last):
    @pl.when(pl.program_id(1) == 0)
    def _():
        x = h_ref[...]
        ms = jnp.mean(x * x, axis=-1, keepdims=True)
        xn_ref[...] = (x * lax.rsqrt(ms + EPS) * g_ref[...]).astype(xn_ref.dtype)

    o_ref[...] = jnp.dot(xn_ref[...], w_ref[...], preferred_element_type=F32).astype(o_ref.dtype)

    slab = jnp.minimum(pl.program_id(0) * nj + pl.program_id(1), last)
    rows = src_a_ref.shape[0]
    shifted = jnp.concatenate([src_a_ref[n_f:rows, :], src_b_ref[0:n_f, :]], axis=0)
    wmain_ref[...] = jnp.where(slab >= qkv_slabs, shifted, src_a_ref[...]).astype(BF16)
    wf_ref[0:n_f, :] = src_f_ref[...].astype(BF16)
    wf_ref[n_f:, :] = jnp.zeros((wf_ref.shape[0] - n_f, wf_ref.shape[1]), BF16)
    aout16_ref[...] = aout_ref[...].astype(BF16)
    bout16_ref[...] = bout_ref[...].astype(BF16)


def _norm_matmul_prep(h, g, w, b_w_in_t, a_w_out, b_w_out, *, n_qkv, n_f, tm, tn, name):
    m, d = h.shape
    n = w.shape[1]
    assert m % tm == 0 and n % tn == 0
    ni, nj = m // tm, n // tn
    steps = ni * nj

    def slab_rows(total_rows, also_divides=None):
        for rows in range(BF16_SUBLANES, total_rows + 1, BF16_SUBLANES):
            if total_rows % rows == 0 and total_rows // rows <= steps and (also_divides is None or also_divides % rows == 0):
                return rows
        raise ValueError("no slab size")

    def index(last):
        return lambda i, j: (jnp.minimum(i * nj + j, last), 0)

    n_main = b_w_in_t.shape[0] - n_f
    assert n_f % 8 == 0 and n_qkv % n_f == 0
    rb = slab_rows(n_main, n_qkv)
    last_b = n_main // rb - 1
    ra = slab_rows(a_w_out.shape[0])
    ro = slab_rows(b_w_out.shape[0])
    kern = functools.partial(_norm_matmul_prep_kernel, qkv_slabs=n_qkv // rb, n_f=n_f, nj=nj, last=last_b)
    return pl.pallas_call(
        kern,
        grid=(ni, nj),
        in_specs=[
            pl.BlockSpec((tm, d), lambda i, j: (i, 0)),
            pl.BlockSpec((1, d), lambda i, j: (0, 0)),
            pl.BlockSpec((d, tn), lambda i, j: (0, j)),
            pl.BlockSpec((rb, d), index(last_b)),
            pl.BlockSpec((rb, d), lambda i, j: (jnp.minimum(i * nj + j, last_b) + 1, 0)),
            pl.BlockSpec((n_f, d), lambda i, j: (n_qkv // n_f, 0)),
            pl.BlockSpec((ra, a_w_out.shape[1]), index(a_w_out.shape[0] // ra - 1)),
            pl.BlockSpec((ro, b_w_out.shape[1]), index(b_w_out.shape[0] // ro - 1)),
        ],
        out_specs=[
            pl.BlockSpec((tm, tn), lambda i, j: (i, j)),
            pl.BlockSpec((rb, d), index(last_b)),
            pl.BlockSpec((V7X_LANES, d), lambda i, j: (0, 0)),
            pl.BlockSpec((ra, a_w_out.shape[1]), index(a_w_out.shape[0] // ra - 1)),
            pl.BlockSpec((ro, b_w_out.shape[1]), index(b_w_out.shape[0] // ro - 1)),
        ],
        out_shape=[
            jax.ShapeDtypeStruct((m, n), BF16),
            jax.ShapeDtypeStruct((n_main, d), BF16),
            jax.ShapeDtypeStruct((V7X_LANES, d), BF16),
            jax.ShapeDtypeStruct(a_w_out.shape, BF16),
            jax.ShapeDtypeStruct(b_w_out.shape, BF16),
        ],
        scratch_shapes=[pltpu.VMEM((tm, d), BF16)],
        compiler_params=_cparams("arbitrary", "arbitrary"),
        name=name,
    )(h, g, w, b_w_in_t, b_w_in_t, b_w_in_t, a_w_out, b_w_out)


def _dot_nt(a, b_t):
    return lax.dot_general(a, b_t, (((1,), (1,)), ((), ())), preferred_element_type=F32)


def _norm_matmul_extra_kernel(h_ref, g_ref, wt_ref, wxt_ref, o_ref, ox_ref, xn_ref):
    @pl.when(pl.program_id(1) == 0)
    def _():
        x = h_ref[...]
        ms = jnp.mean(x * x, axis=-1, keepdims=True)
        xn = (x * lax.rsqrt(ms + EPS) * g_ref[...]).astype(xn_ref.dtype)
        xn_ref[...] = xn
        ox_ref[...] = _dot_nt(xn, wxt_ref[...])

    o_ref[...] = _dot_nt(xn_ref[...], wt_ref[...]).astype(o_ref.dtype)


def _norm_matmul(h, g, w, *, tm, tn, name):
    m, d = h.shape
    n = w.shape[1]
    assert m % tm == 0 and n % tn == 0
    return pl.pallas_call(
        _norm_matmul_kernel,
        grid=(m // tm, n // tn),
        in_specs=[
            pl.BlockSpec((tm, d), lambda i, j: (i, 0)),
            pl.BlockSpec((1, d), lambda i, j: (0, 0)),
            pl.BlockSpec((d, tn), lambda i, j: (0, j)),
        ],
        out_specs=pl.BlockSpec((tm, tn), lambda i, j: (i, j)),
        out_shape=jax.ShapeDtypeStruct((m, n), BF16),
        scratch_shapes=[pltpu.VMEM((tm, d), BF16)],
        compiler_params=_cparams("parallel", "arbitrary"),
        name=name,
    )(h, g, w)


def _norm_matmul_extra(h, g, w_t, wx_t, *, tm, tn, name):
    m, d = h.shape
    n, nx = w_t.shape[0], wx_t.shape[0]
    assert m % tm == 0 and n % tn == 0
    return pl.pallas_call(
        _norm_matmul_extra_kernel,
        grid=(m // tm, n // tn),
        in_specs=[
            pl.BlockSpec((tm, d), lambda i, j: (i, 0)),
            pl.BlockSpec((1, d), lambda i, j: (0, 0)),
            pl.BlockSpec((tn, d), lambda i, j: (j, 0)),
            pl.BlockSpec((nx, d), lambda i, j: (0, 0)),
        ],
        out_specs=[pl.BlockSpec((tm, tn), lambda i, j: (i, j)), pl.BlockSpec((tm, nx), lambda i, j: (i, 0))],
        out_shape=[jax.ShapeDtypeStruct((m, n), BF16), jax.ShapeDtypeStruct((m, nx), F32)],
        scratch_shapes=[pltpu.VMEM((tm, d), BF16)],
        compiler_params=_cparams("parallel", "arbitrary"),
        name=name,
    )(h, g, w_t, wx_t)


def _matmul_residual_kernel(y_ref, w_ref, r_ref, o_ref):
    o_ref[...] = r_ref[...] + jnp.dot(y_ref[...], w_ref[...], preferred_element_type=F32)


def _matmul_residual(y, w, res, *, tm, tn, name):
    m, k = y.shape
    n = w.shape[1]
    assert m % tm == 0 and n % tn == 0
    return pl.pallas_call(
        _matmul_residual_kernel,
        grid=(m // tm, n // tn),
        in_specs=[
            pl.BlockSpec((tm, k), lambda i, j: (i, 0)),
            pl.BlockSpec((k, tn), lambda i, j: (0, j)),
            pl.BlockSpec((tm, tn), lambda i, j: (i, j)),
        ],
        out_specs=pl.BlockSpec((tm, tn), lambda i, j: (i, j)),
        out_shape=jax.ShapeDtypeStruct((m, n), F32),
        compiler_params=_cparams("parallel", "parallel"),
        name=name,
    )(y, w, res)


def _gated_out_proj_kernel(h_ref, o_ref, z_ref, xc_ref, hn_ref, sk_ref, w_ref, r_ref, out_ref, *y_scr,
                           heads_per_step, dv):
    def gate(j):
        cols = slice(j * dv, (j + 1) * dv)
        hg = h_ref[:, cols].astype(F32) * _sigmoid(o_ref[:, cols].astype(F32))
        ms = jnp.mean(hg * hg, axis=-1, keepdims=True)
        hg = hg * lax.rsqrt(ms + EPS) * hn_ref[:, cols]
        z = z_ref[:, cols].astype(F32)
        y = (hg + sk_ref[:, cols] * xc_ref[:, cols].astype(F32)) * _silu(z)
        y_scr[j][...] = y.astype(BF16)

    update = None
    gate(0)
    for j in range(heads_per_step):
        if j + 1 < heads_per_step:
            gate(j + 1)
        part = jnp.dot(y_scr[j][...], w_ref[j * dv:(j + 1) * dv, :], preferred_element_type=F32)
        update = part if update is None else update + part

    @pl.when(pl.program_id(1) == 0)
    def _():
        out_ref[...] = r_ref[...] + update

    @pl.when(pl.program_id(1) != 0)
    def _():
        out_ref[...] += update


def _gated_out_proj(h, proj, xc, hn, sk, w, res, *, tm, heads_per_step, name):
    m, e = h.shape
    d = w.shape[1]
    dv = e // A_HEADS
    kb = heads_per_step * dv
    nk = e // kb
    assert m % tm == 0 and A_HEADS % heads_per_step == 0
    kern = functools.partial(_gated_out_proj_kernel, heads_per_step=heads_per_step, dv=dv)
    return pl.pallas_call(
        kern,
        grid=(m // tm, nk),
        in_specs=[
            pl.BlockSpec((tm, kb), lambda i, k: (i, k)),
            pl.BlockSpec((tm, kb), lambda i, k: (i, nk + k)),
            pl.BlockSpec((tm, kb), lambda i, k: (i, 2 * nk + k)),
            pl.BlockSpec((tm, kb), lambda i, k: (i, k)),
            pl.BlockSpec((1, kb), lambda i, k: (0, k)),
            pl.BlockSpec((1, kb), lambda i, k: (0, k)),
            pl.BlockSpec((kb, d), lambda i, k: (k, 0)),
            pl.BlockSpec((tm, d), lambda i, k: (i, 0)),
        ],
        out_specs=pl.BlockSpec((tm, d), lambda i, k: (i, 0)),
        out_shape=jax.ShapeDtypeStruct((m, d), F32),
        scratch_shapes=[pltpu.VMEM((tm, dv), BF16) for _ in range(heads_per_step)],
        compiler_params=_cparams("parallel", "arbitrary"),
        name=name,
    )(h, proj, proj, xc, hn, sk, w, res)


def _matmul_residual_norm_kernel(y_ref, w_ref, r_ref, g_ref, o_ref):
    x = r_ref[0] + jnp.dot(y_ref[0], w_ref[...], preferred_element_type=F32)
    ms = jnp.mean(x * x, axis=-1, keepdims=True)
    o_ref[0] = x * lax.rsqrt(ms + EPS) * g_ref[...]


def _matmul_residual_norm(y, w, res, g, *, seq, tm, name):
    bsz, _, k = y.shape
    n = w.shape[1]
    assert seq % tm == 0
    return pl.pallas_call(
        _matmul_residual_norm_kernel,
        grid=(bsz, seq // tm),
        in_specs=[
            pl.BlockSpec((1, tm, k), lambda b, i: (b, i, 0)),
            pl.BlockSpec((k, n), lambda b, i: (0, 0)),
            pl.BlockSpec((1, tm, n), lambda b, i: (b, i, 0)),
            pl.BlockSpec((1, n), lambda b, i: (0, 0)),
        ],
        out_specs=pl.BlockSpec((1, tm, n), lambda b, i: (b, i, 0)),
        out_shape=jax.ShapeDtypeStruct((bsz, seq, n), F32),
        compiler_params=_cparams("parallel", "parallel"),
        name=name,
    )(y, w, res, g)


def _conv_qkv_kernel(xm_ref, cw_ref, cb_ref, wq_ref, wk_ref, wv_ref, gq_ref, gk_ref, gv_ref, bg_ref,
                     xc_ref, q_ref, k_ref, v_ref, g_ref, *, seq, n_meta):
    total = seq + n_meta
    blk = CONV_BLOCK
    lead = 8
    half = 16
    group = wq_ref.shape[0]
    hin, dqk, dv = wq_ref.shape[1], wq_ref.shape[2], wv_ref.shape[2]

    @pl.when(pl.program_id(1) == 0)
    def _():
        g_ref[0] = jnp.broadcast_to(bg_ref[...], g_ref.shape[1:])

    def head(g):
        xcols = slice(g * hin, (g + 1) * hin)
        qcols = slice(g * dqk, (g + 1) * dqk)
        vcols = slice(g * dv, (g + 1) * dv)
        cw = cw_ref[:, xcols]
        cb = cb_ref[:, xcols]

        def project(r0, n_rows, xc):
            rows = pl.ds(r0, n_rows)
            xc_ref[0, rows, xcols] = xc
            q = jnp.dot(xc, wq_ref[g], preferred_element_type=F32).astype(BF16)
            k = jnp.dot(xc, wk_ref[g], preferred_element_type=F32).astype(BF16)
            v = jnp.dot(xm_ref[0, rows, xcols], wv_ref[g], preferred_element_type=F32).astype(BF16)
            q_ref[0, rows, qcols] = q
            k_ref[0, rows, qcols] = k
            v_ref[0, rows, vcols] = v
            g_ref[0, rows, :] += (jnp.dot(q, gq_ref[g], preferred_element_type=F32)
                                  + jnp.dot(k, gk_ref[g], preferred_element_type=F32)
                                  + jnp.dot(v, gv_ref[g], preferred_element_type=F32))

        xm_meta = xm_ref[0, seq:total, xcols].astype(F32)
        window = jnp.concatenate([jnp.zeros((lead, hin), F32), xm_meta], axis=0)
        acc = cb
        for s in range(A_CONV):
            acc = acc + window[lead - s:lead - s + n_meta] * cw[A_CONV - 1 - s:A_CONV - s]
        project(seq, n_meta, _silu(acc).astype(BF16))
        yield

        for i in range(seq // blk):
            r0 = i * blk
            c0 = total - half if i == 0 else r0 - half
            ctx = xm_ref[0, c0:c0 + half, xcols].astype(F32)[half - lead:]
            window = jnp.concatenate([ctx, xm_ref[0, r0:r0 + blk, xcols].astype(F32)], axis=0)
            acc = cb
            for s in range(A_CONV):
                acc = acc + window[lead - s:lead - s + blk] * cw[A_CONV - 1 - s:A_CONV - s]
            project(r0, blk, _silu(acc).astype(BF16))
            yield

    gens = [head(g) for g in range(group)]
    while gens:
        gens = [gen for gen in gens if next(gen, "done") != "done"]


def _conv_qkv(proj, cw, cb, wq, wk, wv, gq, gk, gv, bg, *, seq, n_meta):
    bsz, total, _ = proj.shape
    heads, hin, dqk = wq.shape
    dv = wv.shape[2]
    ng = gq.shape[2]
    e = heads * hin
    kern = functools.partial(_conv_qkv_kernel, seq=seq, n_meta=n_meta)
    gp = CONV_HEAD_GROUP
    return pl.pallas_call(
        kern,
        grid=(bsz, heads // gp),
        in_specs=[
            pl.BlockSpec((1, total, gp * hin), lambda b, h: (b, 0, h)),
            pl.BlockSpec((A_CONV, gp * hin), lambda b, h: (0, h)),
            pl.BlockSpec((1, gp * hin), lambda b, h: (0, h)),
            pl.BlockSpec((gp, hin, dqk), lambda b, h: (h, 0, 0)),
            pl.BlockSpec((gp, hin, dqk), lambda b, h: (h, 0, 0)),
            pl.BlockSpec((gp, hin, dv), lambda b, h: (h, 0, 0)),
            pl.BlockSpec((gp, dqk, ng), lambda b, h: (h, 0, 0)),
            pl.BlockSpec((gp, dqk, ng), lambda b, h: (h, 0, 0)),
            pl.BlockSpec((gp, dv, ng), lambda b, h: (h, 0, 0)),
            pl.BlockSpec((1, ng), lambda b, h: (0, 0)),
        ],
        out_specs=[
            pl.BlockSpec((1, total, gp * hin), lambda b, h: (b, 0, h)),
            pl.BlockSpec((1, total, gp * dqk), lambda b, h: (b, 0, h)),
            pl.BlockSpec((1, total, gp * dqk), lambda b, h: (b, 0, h)),
            pl.BlockSpec((1, total, gp * dv), lambda b, h: (b, 0, h)),
            pl.BlockSpec((1, total, ng), lambda b, h: (b, 0, 0)),
        ],
        out_shape=[
            jax.ShapeDtypeStruct((bsz, total, e), BF16),
            jax.ShapeDtypeStruct((bsz, total, heads * dqk), BF16),
            jax.ShapeDtypeStruct((bsz, total, heads * dqk), BF16),
            jax.ShapeDtypeStruct((bsz, total, heads * dv), BF16),
            jax.ShapeDtypeStruct((bsz, total, ng), F32),
        ],
        compiler_params=_cparams("parallel", "arbitrary"),
        name="mlstm_conv_qkv",
    )(proj, cw, cb, wq, wk, wv, gq, gk, gv, bg)


def _mlstm_chunk(q, k, v, ig, f_pre, c_ref, n_ref, m_ref, scale, emit):
    length = q.shape[0]
    ti = lax.broadcasted_iota(jnp.int32, (length, length), 0)
    si = lax.broadcasted_iota(jnp.int32, (length, length), 1)
    eye = ti == si
    b_col, b = _cumsum_row(_log_sigmoid(f_pre), ti, si)
    m_prev = m_ref[...]
    dmat = jnp.where(si <= ti, b_col - b + ig, NEG_INF)
    m_state = b_col + m_prev
    m_t = jnp.maximum(m_state, jnp.max(dmat, axis=1, keepdims=True))
    s_state = jnp.exp(m_state - m_t)
    yield
    qs = (q.astype(F32) * scale).astype(BF16)
    qk = lax.dot_general(qs, k, (((1,), (1,)), ((), ())), preferred_element_type=F32)
    yield
    w = jnp.exp(dmat - m_t) * qk
    yield
    c_old = c_ref[...]
    n_old = n_ref[...]
    num = s_state * jnp.dot(qs, c_old.astype(BF16), preferred_element_type=F32)
    num = num + jnp.dot(w.astype(BF16), v, preferred_element_type=F32)
    yield
    qn = jnp.sum(qs.astype(F32) * n_old, axis=1, keepdims=True)
    den = s_state * qn + jnp.sum(w, axis=1, keepdims=True)
    emit(num / jnp.maximum(jnp.abs(den), jnp.exp(-m_t)))
    yield
    g_end = b[:, length - 1:length]
    decay = g_end - b + ig
    m_new = jnp.maximum(g_end + m_prev, jnp.max(decay, axis=1, keepdims=True))
    s_old = jnp.exp(g_end + m_prev - m_new)
    w_end = _row_to_col(jnp.exp(decay - m_new), eye)
    kw = k.astype(F32) * w_end
    yield
    c_ref[...] = s_old * c_old + lax.dot_general(
        kw.astype(BF16), v, (((0,), (0,)), ((), ())), preferred_element_type=F32)
    n_ref[...] = s_old * n_old + jnp.sum(kw, axis=0, keepdims=True)
    m_ref[...] = m_new


def _mlstm_kernel(q_ref, k_ref, v_ref, igr_ref, fr_ref, igm_ref, fm_ref, h_ref,
                  c_ref, n_ref, m_ref, *, seq, n_meta, chunk, scale, group, dqk, dv):
    c_ref[...] = jnp.zeros_like(c_ref)
    n_ref[...] = jnp.zeros_like(n_ref)
    m_ref[...] = jnp.zeros_like(m_ref)

    def stages(rows, g, ig, f_pre):
        qc = slice(g * dqk, (g + 1) * dqk)
        vc = slice(g * dv, (g + 1) * dv)

        def emit(h):
            h_ref[0, rows, vc] = h.astype(h_ref.dtype)

        return _mlstm_chunk(q_ref[0, rows, qc], k_ref[0, rows, qc], v_ref[0, rows, vc], ig, f_pre,
                            c_ref.at[g], n_ref.at[g], m_ref.at[g], scale, emit)

    def side_by_side(gens):
        gens = list(gens)
        while gens:
            gens = [gen for gen in gens if next(gen, "done") != "done"]

    side_by_side(stages(pl.ds(seq, n_meta), g, igm_ref[0, g], fm_ref[0, g]) for g in range(group))
    for c in range(seq // chunk):
        side_by_side(stages(pl.ds(c * chunk, chunk), g, igr_ref[0, g, c:c + 1, :], fr_ref[0, g, c:c + 1, :])
                     for g in range(group))


def _mlstm(q, k, v, ig_real, f_real, ig_meta, f_meta, *, seq, n_meta):
    bsz, total, _ = q.shape
    heads = A_HEADS
    dqk = q.shape[2] // heads
    dv = v.shape[2] // heads
    e = heads * dv
    chunk = MLSTM_CHUNK
    nc = seq // chunk
    group = MLSTM_HEAD_GROUP
    kern = functools.partial(_mlstm_kernel, seq=seq, n_meta=n_meta, chunk=chunk, scale=dqk ** -0.5,
                             group=group, dqk=dqk, dv=dv)
    return pl.pallas_call(
        kern,
        grid=(bsz, heads // group),
        in_specs=[
            pl.BlockSpec((1, total, group * dqk), lambda b, h: (b, 0, h)),
            pl.BlockSpec((1, total, group * dqk), lambda b, h: (b, 0, h)),
            pl.BlockSpec((1, total, group * dv), lambda b, h: (b, 0, h)),
            pl.BlockSpec((1, group, nc, chunk), lambda b, h: (b, h, 0, 0)),
            pl.BlockSpec((1, group, nc, chunk), lambda b, h: (b, h, 0, 0)),
            pl.BlockSpec((1, group, 1, n_meta), lambda b, h: (b, h, 0, 0)),
            pl.BlockSpec((1, group, 1, n_meta), lambda b, h: (b, h, 0, 0)),
        ],
        out_specs=pl.BlockSpec((1, total, group * dv), lambda b, h: (b, 0, h)),
        out_shape=jax.ShapeDtypeStruct((bsz, total, e), BF16),
        scratch_shapes=[
            pltpu.VMEM((group, dqk, dv), F32),
            pltpu.VMEM((group, 1, dqk), F32),
            pltpu.VMEM((group, 1, 1), F32),
        ],
        compiler_params=_cparams("parallel", "parallel"),
        name="mlstm_recurrence",
    )(q, k, v, ig_real, f_real, ig_meta, f_meta)


def _split3(x):
    hi = x.astype(BF16).astype(F32)
    r1 = x - hi
    mid = r1.astype(BF16).astype(F32)
    return hi, mid, r1 - mid


def _fold_lanes(x, op):
    out = x[:, :V7X_LANES]
    for t in range(1, x.shape[1] // V7X_LANES):
        out = op(out, x[:, t * V7X_LANES:(t + 1) * V7X_LANES])
    return out


def _cumsum_lanes(x):
    length = x.shape[1]
    lane = lax.broadcasted_iota(jnp.int32, x.shape, 1)
    shift = 1
    while shift < length:
        x = x + jnp.where(lane >= shift, pltpu.roll(x, shift, axis=1), 0.0)
        shift *= 2
    return x


def _fox_gates_kernel(bf3_ref, bf2_ref, fr_ref, fm_ref, pr_ref, pm_ref, *, n_meta, qblock):
    heads, nb = fr_ref.shape[1], fr_ref.shape[2]
    lanes = pr_ref.shape[3]
    rows = heads * nb
    assert nb & (nb - 1) == 0 and heads <= lanes
    c_meta = _cumsum_lanes(_log_sigmoid(fm_ref[0] + bf2_ref[...]) * LOG2E)
    hi_ = lax.broadcasted_iota(jnp.int32, (heads, heads), 0)
    hj_ = lax.broadcasted_iota(jnp.int32, (heads, heads), 1)
    meta_row = _col_to_row(c_meta[:, n_meta - 1:n_meta], hi_ == hj_)
    row_head = lax.shift_right_logical(lax.broadcasted_iota(jnp.int32, (rows, heads), 0), nb.bit_length() - 1)
    col_head = lax.broadcasted_iota(jnp.int32, (rows, heads), 1)
    meta_total = jnp.sum(jnp.where(row_head == col_head, meta_row, 0.0), axis=1, keepdims=True)
    c_blk = _cumsum_lanes(_log_sigmoid(fr_ref[0] + bf3_ref[...]).reshape(rows, qblock) * LOG2E)
    ri = lax.broadcasted_iota(jnp.int32, (rows, rows), 0)
    rj = lax.broadcasted_iota(jnp.int32, (rows, rows), 1)
    tot_row = _col_to_row(c_blk[:, qblock - 1:qblock], ri == rj)
    earlier = (rj < ri) & (rj >= ri - (ri & (nb - 1)))
    offset = jnp.sum(jnp.where(earlier, tot_row, 0.0), axis=1, keepdims=True)
    c_rows = c_blk + (offset + meta_total)
    hi, mid, lo = _split3(c_rows)
    pad = jnp.zeros((lanes - 6 * nb, qblock), F32)
    hi_m, mid_m, lo_m = _split3(jnp.concatenate([c_meta, jnp.zeros((lanes - heads, lanes), F32)], axis=0).T)
    lane_m = lax.broadcasted_iota(jnp.int32, (n_meta, lanes), 1)
    for h in range(heads):
        r = slice(h * nb, (h + 1) * nb)
        pr_ref[0, h] = jnp.concatenate([hi[r], mid[r], lo[r], -hi[r], -mid[r], -lo[r], pad], axis=0).T
        a, b, c = hi_m[:n_meta, h:h + 1], mid_m[:n_meta, h:h + 1], lo_m[:n_meta, h:h + 1]
        pm_ref[0, h] = jnp.where(
            lane_m == 0, a, jnp.where(
                lane_m == nb, b, jnp.where(
                    lane_m == 2 * nb, c, jnp.where(
                        lane_m == 3 * nb, -a, jnp.where(
                            lane_m == 4 * nb, -b, jnp.where(lane_m == 5 * nb, -c, 0.0))))))


def _fox_gates(bf, f_real, f_meta, *, n_meta):
    bsz, heads, nb, qblock = f_real.shape
    assert 6 * nb <= V7X_LANES and f_meta.shape[2] == V7X_LANES
    kern = functools.partial(_fox_gates_kernel, n_meta=n_meta, qblock=qblock)
    return pl.pallas_call(
        kern,
        grid=(bsz,),
        in_specs=[
            pl.BlockSpec((heads, 1, 1), lambda b: (0, 0, 0)),
            pl.BlockSpec((heads, 1), lambda b: (0, 0)),
            pl.BlockSpec((1, heads, nb, qblock), lambda b: (b, 0, 0, 0)),
            pl.BlockSpec((1, heads, V7X_LANES), lambda b: (b, 0, 0)),
        ],
        out_specs=[
            pl.BlockSpec((1, heads, qblock, V7X_LANES), lambda b: (b, 0, 0, 0)),
            pl.BlockSpec((1, heads, n_meta, V7X_LANES), lambda b: (b, 0, 0, 0)),
        ],
        out_shape=[
            jax.ShapeDtypeStruct((bsz, heads, qblock, V7X_LANES), F32),
            jax.ShapeDtypeStruct((bsz, heads, n_meta, V7X_LANES), F32),
        ],
        compiler_params=_cparams("parallel"),
        name="fox_gates",
    )(bf.reshape(heads, 1, 1), bf.reshape(heads, 1), f_real, f_meta)


def _fox_kernel(q_ref, k_ref, v_ref, z_ref, qn_ref, kn_ref, pr_ref, pm_ref, y_ref,
                ka_scr, va_scr, *, seq, n_meta, qblock, scale):
    total = seq + n_meta
    nb = seq // qblock
    hd = V7X_LANES
    group = q_ref.shape[2] // hd
    assert 6 * nb <= hd and qblock % V7X_LANES == 0
    qg = qn_ref[...] * (scale * LOG2E)
    kg = kn_ref[...]
    lane = lax.broadcasted_iota(jnp.int32, (qblock, hd), 1)
    lane_m = lax.broadcasted_iota(jnp.int32, (n_meta, hd), 1)
    ti_m = lax.broadcasted_iota(jnp.int32, (n_meta, n_meta), 0)
    si_m = lax.broadcasted_iota(jnp.int32, (n_meta, n_meta), 1)
    ti = lax.broadcasted_iota(jnp.int32, (qblock, qblock), 0)
    si = lax.broadcasted_iota(jnp.int32, (qblock, qblock), 1)
    causal = si <= ti

    def norm(x, g):
        x = x.astype(F32)
        ms = jnp.mean(x * x, axis=-1, keepdims=True)
        return (x * lax.rsqrt(ms + EPS) * g).astype(BF16)

    gens = [_fox_head(q_ref, k_ref, v_ref, z_ref, y_ref, pr_ref[0, g], pm_ref[0, g], ka_scr.at[g], va_scr.at[g],
                      slice(g * hd, (g + 1) * hd), qg, kg, norm, lane, lane_m, si_m <= ti_m, causal,
                      seq=seq, n_meta=n_meta, qblock=qblock) for g in range(group)]
    while gens:
        gens = [gen for gen in gens if next(gen, "done") != "done"]


def _fox_head(q_ref, k_ref, v_ref, z_ref, y_ref, pieces, pieces_m, ka_scr, va_scr, cols, qg, kg, norm,
              lane, lane_m, causal_m, causal, *, seq, n_meta, qblock):
    total = seq + n_meta
    nb = seq // qblock
    hd = V7X_LANES

    def finish(acc, r0, n_rows):
        rows = pl.ds(r0, n_rows)
        z = z_ref[0, rows, cols].astype(F32)
        y_ref[0, rows, cols] = ((acc[:, :hd] / acc[:, hd:]) * _silu(z)).astype(y_ref.dtype)

    def extra(lane_ids, own, slot, *, query):
        lo_lane, ones_lo = (0, 3 * nb) if query else (3 * nb, 0)
        mine = ((lane_ids == lo_lane + slot) | (lane_ids == lo_lane + nb + slot)
                | (lane_ids == lo_lane + 2 * nb + slot))
        ones = (lane_ids >= ones_lo) & (lane_ids < ones_lo + 3 * nb)
        return jnp.where(mine, own, jnp.where(ones, 1.0, 0.0)).astype(BF16)

    def prepare_keys(i):
        r0 = i * qblock
        ka_scr[r0:r0 + qblock, 0:hd] = norm(k_ref[0, r0:r0 + qblock, cols], kg)
        ka_scr[r0:r0 + qblock, hd:2 * hd] = extra(lane, pieces, i, query=False)
        va_scr[r0:r0 + qblock, 0:hd] = v_ref[0, r0:r0 + qblock, cols]
        va_scr[r0:r0 + qblock, hd:2 * hd] = jnp.ones((qblock, hd), BF16)

    ka_scr[seq:total, 0:hd] = norm(k_ref[0, seq:total, cols], kg)
    ka_scr[seq:total, hd:2 * hd] = extra(lane_m, pieces_m, 0, query=False)
    va_scr[seq:total, 0:hd] = v_ref[0, seq:total, cols]
    va_scr[seq:total, hd:2 * hd] = jnp.ones((n_meta, hd), BF16)

    def logits(qa, r0, n_rows):
        return lax.dot_general(qa, ka_scr[r0:r0 + n_rows, :], (((1,), (1,)), ((), ())),
                               preferred_element_type=F32)

    def weighted(p, r0, n_rows):
        return jnp.dot(p.astype(BF16), va_scr[r0:r0 + n_rows, :], preferred_element_type=F32)

    qa = jnp.concatenate([norm(q_ref[0, seq:total, cols], qg), extra(lane_m, pieces_m, 0, query=True)], axis=1)
    s = jnp.where(causal_m, logits(qa, seq, n_meta), NEG_INF)
    p = jnp.exp2(s - jnp.max(s, axis=1, keepdims=True))
    finish(weighted(p, seq, n_meta), seq, n_meta)
    yield

    for j in range(nb):
        r0 = j * qblock
        prepare_keys(j)
        qa = jnp.concatenate([norm(q_ref[0, r0:r0 + qblock, cols], qg), extra(lane, pieces, j, query=True)], axis=1)
        s_m = logits(qa, seq, n_meta)
        parts = []
        m_part = None
        for i in range(j + 1):
            s = logits(qa, i * qblock, qblock)
            if i == j:
                s = jnp.where(causal, s, NEG_INF)
            parts.append(s)
            folded = _fold_lanes(s, jnp.maximum)
            m_part = folded if m_part is None else jnp.maximum(m_part, folded)
        m_run = jnp.maximum(jnp.max(s_m, axis=1, keepdims=True), jnp.max(m_part, axis=1, keepdims=True))
        acc = weighted(jnp.exp2(s_m - m_run), seq, n_meta)
        for i in range(j + 1):
            acc = acc + weighted(jnp.exp2(parts[i] - m_run), i * qblock, qblock)
        finish(acc, r0, qblock)
        yield


def _fox(proj, qn, kn, pieces, pieces_meta, *, seq, n_meta):
    bsz, total, width = proj.shape
    hd = B_HEAD_DIM
    inner = width // 4
    heads = inner // hd
    qblock = FOX_QBLOCK
    group = FOX_HEAD_GROUP
    gw = group * hd
    nblk = heads // group
    kern = functools.partial(_fox_kernel, seq=seq, n_meta=n_meta, qblock=qblock, scale=hd ** -0.5)
    return pl.pallas_call(
        kern,
        grid=(bsz, nblk),
        in_specs=[
            pl.BlockSpec((1, total, gw), lambda b, h: (b, 0, h)),
            pl.BlockSpec((1, total, gw), lambda b, h: (b, 0, nblk + h)),
            pl.BlockSpec((1, total, gw), lambda b, h: (b, 0, 2 * nblk + h)),
            pl.BlockSpec((1, total, gw), lambda b, h: (b, 0, 3 * nblk + h)),
            pl.BlockSpec((1, hd), lambda b, h: (0, 0)),
            pl.BlockSpec((1, hd), lambda b, h: (0, 0)),
            pl.BlockSpec((1, group, qblock, hd), lambda b, h: (b, h, 0, 0)),
            pl.BlockSpec((1, group, n_meta, hd), lambda b, h: (b, h, 0, 0)),
        ],
        out_specs=pl.BlockSpec((1, total, gw), lambda b, h: (b, 0, h)),
        out_shape=jax.ShapeDtypeStruct((bsz, total, inner), BF16),
        scratch_shapes=[pltpu.VMEM((group, total, 2 * hd), BF16), pltpu.VMEM((group, total, 2 * hd), BF16)],
        compiler_params=_cparams("parallel", "parallel"),
        name="fox_attention",
    )(proj, proj, proj, proj, qn, kn, pieces, pieces_meta)


def _gate_rows(g, seq, n_meta, block):
    bsz, _, heads = g.shape
    gt = jnp.transpose(g, (0, 2, 1))
    real = gt[:, :, :seq].reshape(bsz, heads, seq // block, block)
    meta = gt[:, :, seq:].reshape(bsz, heads, 1, n_meta)
    return real, meta


def kernel(x, meta_tokens, a_norm, a_w_in, a_conv_w, a_conv_b, a_w_q, a_w_k, a_w_v, a_w_gate, a_b_gate,
           a_head_norm, a_skip, a_w_out, b_norm, b_w_in, b_b_f, b_q_norm, b_k_norm, b_w_out, final_norm):
    bsz, seq, d = x.shape
    n_meta = meta_tokens.shape[0]
    total = seq + n_meta
    m = bsz * total
    depth_a = a_norm.shape[0]
    depth_b = b_norm.shape[0]
    assert depth_a == 1 and depth_b == 1, "layer schedule: one mLSTM layer followed by one FoX layer"
    tm = total // 3
    assert total % 3 == 0 and tm % 16 == 0

    meta = jnp.broadcast_to(meta_tokens[None].astype(x.dtype), (bsz, n_meta, d))
    h0 = jnp.concatenate([x, meta], axis=1).reshape(m, d)

    heads = A_HEADS
    e = a_w_out.shape[1]
    dqk = a_w_q.shape[3]
    dv = a_w_v.shape[3]
    nqk = heads * dqk
    w_gate = a_w_gate[0].astype(BF16)
    gq = w_gate[:nqk].reshape(heads, dqk, 2 * heads)
    gk = w_gate[nqk:2 * nqk].reshape(heads, dqk, 2 * heads)
    gv = w_gate[2 * nqk:].reshape(heads, dv, 2 * heads)

    inner = b_w_out.shape[1]
    bheads = inner // B_HEAD_DIM
    proj, w_main_t, w_f_t, a_w_out16, b_w_out16 = _norm_matmul_prep(
        h0, a_norm[0][None], a_w_in[0].astype(BF16), b_w_in[0].T, a_w_out[0], b_w_out[0],
        n_qkv=3 * inner, n_f=bheads, tm=2 * tm, tn=1024, name="mlstm_in_proj")
    proj = proj.reshape(bsz, total, 3 * e)
    xc, q, k, v, gates = _conv_qkv(
        proj, a_conv_w[0], a_conv_b[0][None], a_w_q[0].astype(BF16), a_w_k[0].astype(BF16),
        a_w_v[0].astype(BF16), gq, gk, gv, a_b_gate[0][None], seq=seq, n_meta=n_meta)
    ig_real, ig_meta = _gate_rows(gates[..., :heads], seq, n_meta, MLSTM_CHUNK)
    f_real, f_meta = _gate_rows(gates[..., heads:], seq, n_meta, MLSTM_CHUNK)
    hs = _mlstm(q, k, v, ig_real, f_real, ig_meta, f_meta, seq=seq, n_meta=n_meta)
    h1 = _gated_out_proj(hs.reshape(m, e), proj.reshape(m, 3 * e), xc.reshape(m, e), a_head_norm[0][None],
                         a_skip[0][None], a_w_out16, h0, tm=_row_tile(m, 512), heads_per_step=4,
                         name="mlstm_out_proj")

    projb, f_pre = _norm_matmul_extra(h1, b_norm[0][None], w_main_t, w_f_t, tm=2 * tm, tn=1024, name="fox_in_proj")
    fr, fm = _gate_rows(f_pre[:, :bheads].reshape(bsz, total, bheads), seq, n_meta, FOX_QBLOCK)
    fm = jnp.pad(fm.reshape(bsz, bheads, n_meta), ((0, 0), (0, 0), (0, V7X_LANES - n_meta)))
    pieces, pieces_meta = _fox_gates(b_b_f[0], fr, fm, n_meta=n_meta)
    yb = _fox(projb.reshape(bsz, total, 4 * inner), b_q_norm[0][None], b_k_norm[0][None],
              pieces, pieces_meta, seq=seq, n_meta=n_meta)
    return _matmul_residual_norm(yb, b_w_out16, h1.reshape(bsz, total, d), final_norm[None],
                                 seq=seq, tm=512, name="fox_out_proj_norm")
```

```python
import functools

import jax
import jax.numpy as jnp
from jax import lax
from jax.experimental import pallas as pl
from jax.experimental.pallas import tpu as pltpu

F32 = jnp.float32
BF16 = jnp.bfloat16
EPS = 1e-6
NEG_INF = float("-inf")
LOG2E = 1.4426950408889634

V7X_VMEM_BYTES = 64 * 1024 * 1024
V7X_LANES = 128
BF16_SUBLANES = 16
VMEM_LIMIT = V7X_VMEM_BYTES * 7 // 8

A_HEADS = 8
A_CONV = 4
B_HEAD_DIM = 128
MLSTM_CHUNK = 512
MLSTM_HEAD_GROUP = 2
FOX_QBLOCK = 256
FOX_HEAD_GROUP = 4
CONV_BLOCK = 256
CONV_HEAD_GROUP = 2
IN_PROJ_ROWS = 1376
IN_PROJ_COLS = 1024
OUT_PROJ_ROWS = 512
OUT_PROJ_HEADS = 4


def _row_tile(rows, cap):
    for t in range(cap - cap % BF16_SUBLANES, 0, -BF16_SUBLANES):
        if rows % t == 0:
            return t
    raise ValueError(f"no row tile for {rows} rows")


def _cparams(*sem):
    return pltpu.CompilerParams(dimension_semantics=sem, vmem_limit_bytes=VMEM_LIMIT)


def _sigmoid(x):
    return 1.0 / (1.0 + jnp.exp2(x * -LOG2E))


def _silu(x):
    half = 0.5 * x
    return half * (1.0 + jnp.tanh(half))


def _log_sigmoid(x):
    return jnp.minimum(x, 0.0) - jnp.log(1.0 + jnp.exp(-jnp.abs(x)))


def _row_to_col(row, eye):
    return jnp.sum(jnp.where(eye, row, 0.0), axis=1, keepdims=True)


def _col_to_row(col, eye):
    return jnp.sum(jnp.where(eye, col, 0.0), axis=0, keepdims=True)


def _cumsum_row(row, ti, si):
    col = jnp.sum(jnp.where(si <= ti, row, 0.0), axis=1, keepdims=True)
    return col, _col_to_row(col, ti == si)


def _norm_matmul_prep_kernel(h_ref, g_ref, w_ref, src_a_ref, src_b_ref, src_f_ref, aout_ref, bout_ref,
                             o_ref, wmain_ref, wf_ref, aout16_ref, bout16_ref, xn_ref, *, qkv_slabs, n_f, nj, last):
    @pl.when(pl.program_id(1) == 0)
    def _():
        x = h_ref[...]
        ms = jnp.mean(x * x, axis=-1, keepdims=True)
        xn_ref[...] = (x * lax.rsqrt(ms + EPS) * g_ref[...]).astype(xn_ref.dtype)

    o_ref[...] = jnp.dot(xn_ref[...], w_ref[...], preferred_element_type=F32).astype(o_ref.dtype)

    slab = jnp.minimum(pl.program_id(0) * nj + pl.program_id(1), last)
    rows = src_a_ref.shape[0]
    shifted = jnp.concatenate([src_a_ref[n_f:rows, :], src_b_ref[0:n_f, :]], axis=0)
    wmain_ref[...] = jnp.where(slab >= qkv_slabs, shifted, src_a_ref[...]).astype(BF16)
    wf_ref[0:n_f, :] = src_f_ref[...].astype(BF16)
    wf_ref[n_f:, :] = jnp.zeros((wf_ref.shape[0] - n_f, wf_ref.shape[1]), BF16)
    aout16_ref[...] = aout_ref[...].astype(BF16)
    bout16_ref[...] = bout_ref[...].astype(BF16)


def _norm_matmul_prep(h, g, w, b_w_in_t, a_w_out, b_w_out, *, n_qkv, n_f, tm, tn, name):
    m, d = h.shape
    n = w.shape[1]
    assert m % tm == 0 and n % tn == 0
    ni, nj = m // tm, n // tn
    steps = ni * nj

    def slab_rows(total_rows, also_divides=None):
        for rows in range(BF16_SUBLANES, total_rows + 1, BF16_SUBLANES):
            if total_rows % rows == 0 and total_rows // rows <= steps and (also_divides is None or also_divides % rows == 0):
                return rows
        raise ValueError("no slab size")

    def index(last):
        return lambda i, j: (jnp.minimum(i * nj + j, last), 0)

    n_main = b_w_in_t.shape[0] - n_f
    assert n_f % 8 == 0 and n_qkv % n_f == 0
    rb = slab_rows(n_main, n_qkv)
    last_b = n_main // rb - 1
    ra = slab_rows(a_w_out.shape[0])
    ro = slab_rows(b_w_out.shape[0])
    kern = functools.partial(_norm_matmul_prep_kernel, qkv_slabs=n_qkv // rb, n_f=n_f, nj=nj, last=last_b)
    return pl.pallas_call(
        kern,
        grid=(ni, nj),
        in_specs=[
            pl.BlockSpec((tm, d), lambda i, j: (i, 0)),
            pl.BlockSpec((1, d), lambda i, j: (0, 0)),
            pl.BlockSpec((d, tn), lambda i, j: (0, j)),
            pl.BlockSpec((rb, d), index(last_b)),
            pl.BlockSpec((rb, d), lambda i, j: (jnp.minimum(i * nj + j, last_b) + 1, 0)),
            pl.BlockSpec((n_f, d), lambda i, j: (n_qkv // n_f, 0)),
            pl.BlockSpec((ra, a_w_out.shape[1]), index(a_w_out.shape[0] // ra - 1)),
            pl.BlockSpec((ro, b_w_out.shape[1]), index(b_w_out.shape[0] // ro - 1)),
        ],
        out_specs=[
            pl.BlockSpec((tm, tn), lambda i, j: (i, j)),
            pl.BlockSpec((rb, d), index(last_b)),
            pl.BlockSpec((V7X_LANES, d), lambda i, j: (0, 0)),
            pl.BlockSpec((ra, a_w_out.shape[1]), index(a_w_out.shape[0] // ra - 1)),
            pl.BlockSpec((ro, b_w_out.shape[1]), index(b_w_out.shape[0] // ro - 1)),
        ],
        out_shape=[
            jax.ShapeDtypeStruct((m, n), BF16),
            jax.ShapeDtypeStruct((n_main, d), BF16),
            jax.ShapeDtypeStruct((V7X_LANES, d), BF16),
            jax.ShapeDtypeStruct(a_w_out.shape, BF16),
            jax.ShapeDtypeStruct(b_w_out.shape, BF16),
        ],
        scratch_shapes=[pltpu.VMEM((tm, d), BF16)],
        compiler_params=_cparams("arbitrary", "arbitrary"),
        name=name,
    )(h, g, w, b_w_in_t, b_w_in_t, b_w_in_t, a_w_out, b_w_out)


def _dot_nt(a, b_t):
    return lax.dot_general(a, b_t, (((1,), (1,)), ((), ())), preferred_element_type=F32)


def _norm_matmul_extra_kernel(h_ref, g_ref, wt_ref, wxt_ref, o_ref, ox_ref, xn_ref):
    @pl.when(pl.program_id(1) == 0)
    def _():
        x = h_ref[...]
        ms = jnp.mean(x * x, axis=-1, keepdims=True)
        xn = (x * lax.rsqrt(ms + EPS) * g_ref[...]).astype(xn_ref.dtype)
        xn_ref[...] = xn
        ox_ref[...] = _dot_nt(xn, wxt_ref[...])

    o_ref[...] = _dot_nt(xn_ref[...], wt_ref[...]).astype(o_ref.dtype)


def _norm_matmul_extra(h, g, w_t, wx_t, *, tm, tn, name):
    m, d = h.shape
    n, nx = w_t.shape[0], wx_t.shape[0]
    assert m % tm == 0 and n % tn == 0
    return pl.pallas_call(
        _norm_matmul_extra_kernel,
        grid=(m // tm, n // tn),
        in_specs=[
            pl.BlockSpec((tm, d), lambda i, j: (i, 0)),
            pl.BlockSpec((1, d), lambda i, j: (0, 0)),
            pl.BlockSpec((tn, d), lambda i, j: (j, 0)),
            pl.BlockSpec((nx, d), lambda i, j: (0, 0)),
        ],
        out_specs=[pl.BlockSpec((tm, tn), lambda i, j: (i, j)), pl.BlockSpec((tm, nx), lambda i, j: (i, 0))],
        out_shape=[jax.ShapeDtypeStruct((m, n), BF16), jax.ShapeDtypeStruct((m, nx), F32)],
        scratch_shapes=[pltpu.VMEM((tm, d), BF16)],
        compiler_params=_cparams("parallel", "arbitrary"),
        name=name,
    )(h, g, w_t, wx_t)


def _gated_out_proj_kernel(h_ref, o_ref, z_ref, xc_ref, hn_ref, sk_ref, w_ref, r_ref, out_ref, *y_scr,
                           heads_per_step, dv):
    def gate(j):
        cols = slice(j * dv, (j + 1) * dv)
        hg = h_ref[:, cols].astype(F32) * _sigmoid(o_ref[:, cols].astype(F32))
        ms = jnp.mean(hg * hg, axis=-1, keepdims=True)
        hg = hg * lax.rsqrt(ms + EPS) * hn_ref[:, cols]
        z = z_ref[:, cols].astype(F32)
        y = (hg + sk_ref[:, cols] * xc_ref[:, cols].astype(F32)) * _silu(z)
        y_scr[j][...] = y.astype(BF16)

    update = None
    gate(0)
    for j in range(heads_per_step):
        if j + 1 < heads_per_step:
            gate(j + 1)
        part = jnp.dot(y_scr[j][...], w_ref[j * dv:(j + 1) * dv, :], preferred_element_type=F32)
        update = part if update is None else update + part

    @pl.when(pl.program_id(1) == 0)
    def _():
        out_ref[...] = r_ref[...] + update

    @pl.when(pl.program_id(1) != 0)
    def _():
        out_ref[...] += update


def _gated_out_proj(h, proj, xc, hn, sk, w, res, *, tm, heads_per_step, name):
    m, e = h.shape
    d = w.shape[1]
    dv = e // A_HEADS
    kb = heads_per_step * dv
    nk = e // kb
    assert m % tm == 0 and A_HEADS % heads_per_step == 0
    kern = functools.partial(_gated_out_proj_kernel, heads_per_step=heads_per_step, dv=dv)
    return pl.pallas_call(
        kern,
        grid=(m // tm, nk),
        in_specs=[
            pl.BlockSpec((tm, kb), lambda i, k: (i, k)),
            pl.BlockSpec((tm, kb), lambda i, k: (i, nk + k)),
            pl.BlockSpec((tm, kb), lambda i, k: (i, 2 * nk + k)),
            pl.BlockSpec((tm, kb), lambda i, k: (i, k)),
            pl.BlockSpec((1, kb), lambda i, k: (0, k)),
            pl.BlockSpec((1, kb), lambda i, k: (0, k)),
            pl.BlockSpec((kb, d), lambda i, k: (k, 0)),
            pl.BlockSpec((tm, d), lambda i, k: (i, 0)),
        ],
        out_specs=pl.BlockSpec((tm, d), lambda i, k: (i, 0)),
        out_shape=jax.ShapeDtypeStruct((m, d), F32),
        scratch_shapes=[pltpu.VMEM((tm, dv), BF16) for _ in range(heads_per_step)],
        compiler_params=_cparams("parallel", "arbitrary"),
        name=name,
    )(h, proj, proj, xc, hn, sk, w, res)


def _matmul_residual_norm_kernel(y_ref, w_ref, r_ref, g_ref, o_ref):
    x = r_ref[0] + jnp.dot(y_ref[0], w_ref[...], preferred_element_type=F32)
    ms = jnp.mean(x * x, axis=-1, keepdims=True)
    o_ref[0] = x * lax.rsqrt(ms + EPS) * g_ref[...]


def _matmul_residual_norm(y, w, res, g, *, seq, tm, name):
    bsz, _, k = y.shape
    n = w.shape[1]
    assert seq % tm == 0
    return pl.pallas_call(
        _matmul_residual_norm_kernel,
        grid=(bsz, seq // tm),
        in_specs=[
            pl.BlockSpec((1, tm, k), lambda b, i: (b, i, 0)),
            pl.BlockSpec((k, n), lambda b, i: (0, 0)),
            pl.BlockSpec((1, tm, n), lambda b, i: (b, i, 0)),
            pl.BlockSpec((1, n), lambda b, i: (0, 0)),
        ],
        out_specs=pl.BlockSpec((1, tm, n), lambda b, i: (b, i, 0)),
        out_shape=jax.ShapeDtypeStruct((bsz, seq, n), F32),
        compiler_params=_cparams("parallel", "parallel"),
        name=name,
    )(y, w, res, g)


def _conv_qkv_kernel(xm_ref, cw_ref, cb_ref, wq_ref, wk_ref, wv_ref, gq_ref, gk_ref, gv_ref, bg_ref,
                     xc_ref, q_ref, k_ref, v_ref, g_ref, *, seq, n_meta):
    total = seq + n_meta
    blk = CONV_BLOCK
    lead = 8
    half = 16
    group = wq_ref.shape[0]
    hin, dqk, dv = wq_ref.shape[1], wq_ref.shape[2], wv_ref.shape[2]

    @pl.when(pl.program_id(1) == 0)
    def _():
        g_ref[0] = jnp.broadcast_to(bg_ref[...], g_ref.shape[1:])

    def head(g):
        xcols = slice(g * hin, (g + 1) * hin)
        qcols = slice(g * dqk, (g + 1) * dqk)
        vcols = slice(g * dv, (g + 1) * dv)
        cw = cw_ref[:, xcols]
        cb = cb_ref[:, xcols]

        def project(r0, n_rows, xc):
            rows = pl.ds(r0, n_rows)
            xc_ref[0, rows, xcols] = xc
            q = jnp.dot(xc, wq_ref[g], preferred_element_type=F32).astype(BF16)
            k = jnp.dot(xc, wk_ref[g], preferred_element_type=F32).astype(BF16)
            v = jnp.dot(xm_ref[0, rows, xcols], wv_ref[g], preferred_element_type=F32).astype(BF16)
            q_ref[0, rows, qcols] = q
            k_ref[0, rows, qcols] = k
            v_ref[0, rows, vcols] = v
            g_ref[0, rows, :] += (jnp.dot(q, gq_ref[g], preferred_element_type=F32)
                                  + jnp.dot(k, gk_ref[g], preferred_element_type=F32)
                                  + jnp.dot(v, gv_ref[g], preferred_element_type=F32))

        xm_meta = xm_ref[0, seq:total, xcols].astype(F32)
        window = jnp.concatenate([jnp.zeros((lead, hin), F32), xm_meta], axis=0)
        acc = cb
        for s in range(A_CONV):
            acc = acc + window[lead - s:lead - s + n_meta] * cw[A_CONV - 1 - s:A_CONV - s]
        project(seq, n_meta, _silu(acc).astype(BF16))
        yield

        for i in range(seq // blk):
            r0 = i * blk
            c0 = total - half if i == 0 else r0 - half
            ctx = xm_ref[0, c0:c0 + half, xcols].astype(F32)[half - lead:]
            window = jnp.concatenate([ctx, xm_ref[0, r0:r0 + blk, xcols].astype(F32)], axis=0)
            acc = cb
            for s in range(A_CONV):
                acc = acc + window[lead - s:lead - s + blk] * cw[A_CONV - 1 - s:A_CONV - s]
            project(r0, blk, _silu(acc).astype(BF16))
            yield

    gens = [head(g) for g in range(group)]
    while gens:
        gens = [gen for gen in gens if next(gen, "done") != "done"]


def _conv_qkv(proj, cw, cb, wq, wk, wv, gq, gk, gv, bg, *, seq, n_meta):
    bsz, total, _ = proj.shape
    heads, hin, dqk = wq.shape
    dv = wv.shape[2]
    ng = gq.shape[2]
    e = heads * hin
    kern = functools.partial(_conv_qkv_kernel, seq=seq, n_meta=n_meta)
    gp = CONV_HEAD_GROUP
    return pl.pallas_call(
        kern,
        grid=(bsz, heads // gp),
        in_specs=[
            pl.BlockSpec((1, total, gp * hin), lambda b, h: (b, 0, h)),
            pl.BlockSpec((A_CONV, gp * hin), lambda b, h: (0, h)),
            pl.BlockSpec((1, gp * hin), lambda b, h: (0, h)),
            pl.BlockSpec((gp, hin, dqk), lambda b, h: (h, 0, 0)),
            pl.BlockSpec((gp, hin, dqk), lambda b, h: (h, 0, 0)),
            pl.BlockSpec((gp, hin, dv), lambda b, h: (h, 0, 0)),
            pl.BlockSpec((gp, dqk, ng), lambda b, h: (h, 0, 0)),
            pl.BlockSpec((gp, dqk, ng), lambda b, h: (h, 0, 0)),
            pl.BlockSpec((gp, dv, ng), lambda b, h: (h, 0, 0)),
            pl.BlockSpec((1, ng), lambda b, h: (0, 0)),
        ],
        out_specs=[
            pl.BlockSpec((1, total, gp * hin), lambda b, h: (b, 0, h)),
            pl.BlockSpec((1, total, gp * dqk), lambda b, h: (b, 0, h)),
            pl.BlockSpec((1, total, gp * dqk), lambda b, h: (b, 0, h)),
            pl.BlockSpec((1, total, gp * dv), lambda b, h: (b, 0, h)),
            pl.BlockSpec((1, total, ng), lambda b, h: (b, 0, 0)),
        ],
        out_shape=[
            jax.ShapeDtypeStruct((bsz, total, e), BF16),
            jax.ShapeDtypeStruct((bsz, total, heads * dqk), BF16),
            jax.ShapeDtypeStruct((bsz, total, heads * dqk), BF16),
            jax.ShapeDtypeStruct((bsz, total, heads * dv), BF16),
            jax.ShapeDtypeStruct((bsz, total, ng), F32),
        ],
        compiler_params=_cparams("parallel", "arbitrary"),
        name="mlstm_conv_qkv",
    )(proj, cw, cb, wq, wk, wv, gq, gk, gv, bg)


def _mlstm_chunk(q, k, v, ig, f_pre, c_ref, n_ref, m_ref, scale, emit):
    length = q.shape[0]
    ti = lax.broadcasted_iota(jnp.int32, (length, length), 0)
    si = lax.broadcasted_iota(jnp.int32, (length, length), 1)
    eye = ti == si
    b_col, b = _cumsum_row(_log_sigmoid(f_pre), ti, si)
    m_prev = m_ref[...]
    dmat = jnp.where(si <= ti, b_col - b + ig, NEG_INF)
    m_state = b_col + m_prev
    m_t = jnp.maximum(m_state, jnp.max(dmat, axis=1, keepdims=True))
    s_state = jnp.exp(m_state - m_t)
    yield
    qs = (q.astype(F32) * scale).astype(BF16)
    qk = lax.dot_general(qs, k, (((1,), (1,)), ((), ())), preferred_element_type=F32)
    yield
    w = jnp.exp(dmat - m_t) * qk
    yield
    c_old = c_ref[...]
    n_old = n_ref[...]
    num = s_state * jnp.dot(qs, c_old.astype(BF16), preferred_element_type=F32)
    num = num + jnp.dot(w.astype(BF16), v, preferred_element_type=F32)
    yield
    qn = jnp.sum(qs.astype(F32) * n_old, axis=1, keepdims=True)
    den = s_state * qn + jnp.sum(w, axis=1, keepdims=True)
    emit(num / jnp.maximum(jnp.abs(den), jnp.exp(-m_t)))
    yield
    g_end = b[:, length - 1:length]
    decay = g_end - b + ig
    m_new = jnp.maximum(g_end + m_prev, jnp.max(decay, axis=1, keepdims=True))
    s_old = jnp.exp(g_end + m_prev - m_new)
    w_end = _row_to_col(jnp.exp(decay - m_new), eye)
    kw = k.astype(F32) * w_end
    yield
    c_ref[...] = s_old * c_old + lax.dot_general(
        kw.astype(BF16), v, (((0,), (0,)), ((), ())), preferred_element_type=F32)
    n_ref[...] = s_old * n_old + jnp.sum(kw, axis=0, keepdims=True)
    m_ref[...] = m_new


def _mlstm_kernel(q_ref, k_ref, v_ref, igr_ref, fr_ref, igm_ref, fm_ref, h_ref,
                  c_ref, n_ref, m_ref, *, seq, n_meta, chunk, scale, group, dqk, dv):
    c_ref[...] = jnp.zeros_like(c_ref)
    n_ref[...] = jnp.zeros_like(n_ref)
    m_ref[...] = jnp.zeros_like(m_ref)

    def stages(rows, g, ig, f_pre):
        qc = slice(g * dqk, (g + 1) * dqk)
        vc = slice(g * dv, (g + 1) * dv)

        def emit(h):
            h_ref[0, rows, vc] = h.astype(h_ref.dtype)

        return _mlstm_chunk(q_ref[0, rows, qc], k_ref[0, rows, qc], v_ref[0, rows, vc], ig, f_pre,
                            c_ref.at[g], n_ref.at[g], m_ref.at[g], scale, emit)

    def side_by_side(gens):
        gens = list(gens)
        while gens:
            gens = [gen for gen in gens if next(gen, "done") != "done"]

    side_by_side(stages(pl.ds(seq, n_meta), g, igm_ref[0, g], fm_ref[0, g]) for g in range(group))
    for c in range(seq // chunk):
        side_by_side(stages(pl.ds(c * chunk, chunk), g, igr_ref[0, g, c:c + 1, :], fr_ref[0, g, c:c + 1, :])
                     for g in range(group))


def _mlstm(q, k, v, ig_real, f_real, ig_meta, f_meta, *, seq, n_meta):
    bsz, total, _ = q.shape
    heads = A_HEADS
    dqk = q.shape[2] // heads
    dv = v.shape[2] // heads
    e = heads * dv
    chunk = MLSTM_CHUNK
    nc = seq // chunk
    group = MLSTM_HEAD_GROUP
    kern = functools.partial(_mlstm_kernel, seq=seq, n_meta=n_meta, chunk=chunk, scale=dqk ** -0.5,
                             group=group, dqk=dqk, dv=dv)
    return pl.pallas_call(
        kern,
        grid=(bsz, heads // group),
        in_specs=[
            pl.BlockSpec((1, total, group * dqk), lambda b, h: (b, 0, h)),
            pl.BlockSpec((1, total, group * dqk), lambda b, h: (b, 0, h)),
            pl.BlockSpec((1, total, group * dv), lambda b, h: (b, 0, h)),
            pl.BlockSpec((1, group, nc, chunk), lambda b, h: (b, h, 0, 0)),
            pl.BlockSpec((1, group, nc, chunk), lambda b, h: (b, h, 0, 0)),
            pl.BlockSpec((1, group, 1, n_meta), lambda b, h: (b, h, 0, 0)),
            pl.BlockSpec((1, group, 1, n_meta), lambda b, h: (b, h, 0, 0)),
        ],
        out_specs=pl.BlockSpec((1, total, group * dv), lambda b, h: (b, 0, h)),
        out_shape=jax.ShapeDtypeStruct((bsz, total, e), BF16),
        scratch_shapes=[
            pltpu.VMEM((group, dqk, dv), F32),
            pltpu.VMEM((group, 1, dqk), F32),
            pltpu.VMEM((group, 1, 1), F32),
        ],
        compiler_params=_cparams("parallel", "parallel"),
        name="mlstm_recurrence",
    )(q, k, v, ig_real, f_real, ig_meta, f_meta)


def _split3(x):
    hi = x.astype(BF16).astype(F32)
    r1 = x - hi
    mid = r1.astype(BF16).astype(F32)
    return hi, mid, r1 - mid


def _fold_lanes(x, op):
    out = x[:, :V7X_LANES]
    for t in range(1, x.shape[1] // V7X_LANES):
        out = op(out, x[:, t * V7X_LANES:(t + 1) * V7X_LANES])
    return out


def _cumsum_lanes(x):
    length = x.shape[1]
    lane = lax.broadcasted_iota(jnp.int32, x.shape, 1)
    shift = 1
    while shift < length:
        x = x + jnp.where(lane >= shift, pltpu.roll(x, shift, axis=1), 0.0)
        shift *= 2
    return x


def _fox_gates_kernel(bf3_ref, bf2_ref, fr_ref, fm_ref, pr_ref, pm_ref, *, n_meta, qblock):
    heads, nb = fr_ref.shape[1], fr_ref.shape[2]
    lanes = pr_ref.shape[3]
    rows = heads * nb
    assert nb & (nb - 1) == 0 and heads <= lanes
    c_meta = _cumsum_lanes(_log_sigmoid(fm_ref[0] + bf2_ref[...]) * LOG2E)
    hi_ = lax.broadcasted_iota(jnp.int32, (heads, heads), 0)
    hj_ = lax.broadcasted_iota(jnp.int32, (heads, heads), 1)
    meta_row = _col_to_row(c_meta[:, n_meta - 1:n_meta], hi_ == hj_)
    row_head = lax.shift_right_logical(lax.broadcasted_iota(jnp.int32, (rows, heads), 0), nb.bit_length() - 1)
    col_head = lax.broadcasted_iota(jnp.int32, (rows, heads), 1)
    meta_total = jnp.sum(jnp.where(row_head == col_head, meta_row, 0.0), axis=1, keepdims=True)
    c_blk = _cumsum_lanes(_log_sigmoid(fr_ref[0] + bf3_ref[...]).reshape(rows, qblock) * LOG2E)
    ri = lax.broadcasted_iota(jnp.int32, (rows, rows), 0)
    rj = lax.broadcasted_iota(jnp.int32, (rows, rows), 1)
    tot_row = _col_to_row(c_blk[:, qblock - 1:qblock], ri == rj)
    earlier = (rj < ri) & (rj >= ri - (ri & (nb - 1)))
    offset = jnp.sum(jnp.where(earlier, tot_row, 0.0), axis=1, keepdims=True)
    c_rows = c_blk + (offset + meta_total)
    hi, mid, lo = _split3(c_rows)
    pad = jnp.zeros((lanes - 6 * nb, qblock), F32)
    hi_m, mid_m, lo_m = _split3(jnp.concatenate([c_meta, jnp.zeros((lanes - heads, lanes), F32)], axis=0).T)
    lane_m = lax.broadcasted_iota(jnp.int32, (n_meta, lanes), 1)
    for h in range(heads):
        r = slice(h * nb, (h + 1) * nb)
        pr_ref[0, h] = jnp.concatenate([hi[r], mid[r], lo[r], -hi[r], -mid[r], -lo[r], pad], axis=0).T
        a, b, c = hi_m[:n_meta, h:h + 1], mid_m[:n_meta, h:h + 1], lo_m[:n_meta, h:h + 1]
        pm_ref[0, h] = jnp.where(
            lane_m == 0, a, jnp.where(
                lane_m == nb, b, jnp.where(
                    lane_m == 2 * nb, c, jnp.where(
                        lane_m == 3 * nb, -a, jnp.where(
                            lane_m == 4 * nb, -b, jnp.where(lane_m == 5 * nb, -c, 0.0))))))


def _fox_gates(bf, f_real, f_meta, *, n_meta):
    bsz, heads, nb, qblock = f_real.shape
    assert 6 * nb <= V7X_LANES and f_meta.shape[2] == V7X_LANES
    kern = functools.partial(_fox_gates_kernel, n_meta=n_meta, qblock=qblock)
    return pl.pallas_call(
        kern,
        grid=(bsz,),
        in_specs=[
            pl.BlockSpec((heads, 1, 1), lambda b: (0, 0, 0)),
            pl.BlockSpec((heads, 1), lambda b: (0, 0)),
            pl.BlockSpec((1, heads, nb, qblock), lambda b: (b, 0, 0, 0)),
            pl.BlockSpec((1, heads, V7X_LANES), lambda b: (b, 0, 0)),
        ],
        out_specs=[
            pl.BlockSpec((1, heads, qblock, V7X_LANES), lambda b: (b, 0, 0, 0)),
            pl.BlockSpec((1, heads, n_meta, V7X_LANES), lambda b: (b, 0, 0, 0)),
        ],
        out_shape=[
            jax.ShapeDtypeStruct((bsz, heads, qblock, V7X_LANES), F32),
            jax.ShapeDtypeStruct((bsz, heads, n_meta, V7X_LANES), F32),
        ],
        compiler_params=_cparams("parallel"),
        name="fox_gates",
    )(bf.reshape(heads, 1, 1), bf.reshape(heads, 1), f_real, f_meta)


def _fox_kernel(q_ref, k_ref, v_ref, z_ref, qn_ref, kn_ref, pr_ref, pm_ref, y_ref,
                ka_scr, va_scr, *, seq, n_meta, qblock, scale):
    total = seq + n_meta
    nb = seq // qblock
    hd = V7X_LANES
    group = q_ref.shape[2] // hd
    assert 6 * nb <= hd and qblock % V7X_LANES == 0
    qg = qn_ref[...] * (scale * LOG2E)
    kg = kn_ref[...]
    lane = lax.broadcasted_iota(jnp.int32, (qblock, hd), 1)
    lane_m = lax.broadcasted_iota(jnp.int32, (n_meta, hd), 1)
    ti_m = lax.broadcasted_iota(jnp.int32, (n_meta, n_meta), 0)
    si_m = lax.broadcasted_iota(jnp.int32, (n_meta, n_meta), 1)
    ti = lax.broadcasted_iota(jnp.int32, (qblock, qblock), 0)
    si = lax.broadcasted_iota(jnp.int32, (qblock, qblock), 1)
    causal = si <= ti

    def norm(x, g):
        x = x.astype(F32)
        ms = jnp.mean(x * x, axis=-1, keepdims=True)
        return (x * lax.rsqrt(ms + EPS) * g).astype(BF16)

    gens = [_fox_head(q_ref, k_ref, v_ref, z_ref, y_ref, pr_ref[0, g], pm_ref[0, g], ka_scr.at[g], va_scr.at[g],
                      slice(g * hd, (g + 1) * hd), qg, kg, norm, lane, lane_m, si_m <= ti_m, causal,
                      seq=seq, n_meta=n_meta, qblock=qblock) for g in range(group)]
    while gens:
        gens = [gen for gen in gens if next(gen, "done") != "done"]


def _fox_head(q_ref, k_ref, v_ref, z_ref, y_ref, pieces, pieces_m, ka_scr, va_scr, cols, qg, kg, norm,
              lane, lane_m, causal_m, causal, *, seq, n_meta, qblock):
    total = seq + n_meta
    nb = seq // qblock
    hd = V7X_LANES

    def finish(acc, r0, n_rows):
        rows = pl.ds(r0, n_rows)
        z = z_ref[0, rows, cols].astype(F32)
        y_ref[0, rows, cols] = ((acc[:, :hd] / acc[:, hd:]) * _silu(z)).astype(y_ref.dtype)

    def extra(lane_ids, own, slot, *, query):
        lo_lane, ones_lo = (0, 3 * nb) if query else (3 * nb, 0)
        mine = ((lane_ids == lo_lane + slot) | (lane_ids == lo_lane + nb + slot)
                | (lane_ids == lo_lane + 2 * nb + slot))
        ones = (lane_ids >= ones_lo) & (lane_ids < ones_lo + 3 * nb)
        return jnp.where(mine, own, jnp.where(ones, 1.0, 0.0)).astype(BF16)

    def prepare_keys(i):
        r0 = i * qblock
        ka_scr[r0:r0 + qblock, 0:hd] = norm(k_ref[0, r0:r0 + qblock, cols], kg)
        ka_scr[r0:r0 + qblock, hd:2 * hd] = extra(lane, pieces, i, query=False)
        va_scr[r0:r0 + qblock, 0:hd] = v_ref[0, r0:r0 + qblock, cols]
        va_scr[r0:r0 + qblock, hd:2 * hd] = jnp.ones((qblock, hd), BF16)

    ka_scr[seq:total, 0:hd] = norm(k_ref[0, seq:total, cols], kg)
    ka_scr[seq:total, hd:2 * hd] = extra(lane_m, pieces_m, 0, query=False)
    va_scr[seq:total, 0:hd] = v_ref[0, seq:total, cols]
    va_scr[seq:total, hd:2 * hd] = jnp.ones((n_meta, hd), BF16)

    def logits(qa, r0, n_rows):
        return lax.dot_general(qa, ka_scr[r0:r0 + n_rows, :], (((1,), (1,)), ((), ())),
                               preferred_element_type=F32)

    def weighted(p, r0, n_rows):
        return jnp.dot(p.astype(BF16), va_scr[r0:r0 + n_rows, :], preferred_element_type=F32)

    qa = jnp.concatenate([norm(q_ref[0, seq:total, cols], qg), extra(lane_m, pieces_m, 0, query=True)], axis=1)
    s = jnp.where(causal_m, logits(qa, seq, n_meta), NEG_INF)
    p = jnp.exp2(s - jnp.max(s, axis=1, keepdims=True))
    finish(weighted(p, seq, n_meta), seq, n_meta)
    yield

    for j in range(nb):
        r0 = j * qblock
        prepare_keys(j)
        qa = jnp.concatenate([norm(q_ref[0, r0:r0 + qblock, cols], qg), extra(lane, pieces, j, query=True)], axis=1)
        s_m = logits(qa, seq, n_meta)
        parts = []
        m_part = None
        for i in range(j + 1):
            s = logits(qa, i * qblock, qblock)
            if i == j:
                s = jnp.where(causal, s, NEG_INF)
            parts.append(s)
            folded = _fold_lanes(s, jnp.maximum)
            m_part = folded if m_part is None else jnp.maximum(m_part, folded)
        m_run = jnp.maximum(jnp.max(s_m, axis=1, keepdims=True), jnp.max(m_part, axis=1, keepdims=True))
        acc = weighted(jnp.exp2(s_m - m_run), seq, n_meta)
        for i in range(j + 1):
            acc = acc + weighted(jnp.exp2(parts[i] - m_run), i * qblock, qblock)
        finish(acc, r0, qblock)
        yield


def _fox(proj, qn, kn, pieces, pieces_meta, *, seq, n_meta):
    bsz, total, width = proj.shape
    hd = B_HEAD_DIM
    inner = width // 4
    heads = inner // hd
    qblock = FOX_QBLOCK
    group = FOX_HEAD_GROUP
    gw = group * hd
    nblk = heads // group
    kern = functools.partial(_fox_kernel, seq=seq, n_meta=n_meta, qblock=qblock, scale=hd ** -0.5)
    return pl.pallas_call(
        kern,
        grid=(bsz, nblk),
        in_specs=[
            pl.BlockSpec((1, total, gw), lambda b, h: (b, 0, h)),
            pl.BlockSpec((1, total, gw), lambda b, h: (b, 0, nblk + h)),
            pl.BlockSpec((1, total, gw), lambda b, h: (b, 0, 2 * nblk + h)),
            pl.BlockSpec((1, total, gw), lambda b, h: (b, 0, 3 * nblk + h)),
            pl.BlockSpec((1, hd), lambda b, h: (0, 0)),
            pl.BlockSpec((1, hd), lambda b, h: (0, 0)),
            pl.BlockSpec((1, group, qblock, hd), lambda b, h: (b, h, 0, 0)),
            pl.BlockSpec((1, group, n_meta, hd), lambda b, h: (b, h, 0, 0)),
        ],
        out_specs=pl.BlockSpec((1, total, gw), lambda b, h: (b, 0, h)),
        out_shape=jax.ShapeDtypeStruct((bsz, total, inner), BF16),
        scratch_shapes=[pltpu.VMEM((group, total, 2 * hd), BF16), pltpu.VMEM((group, total, 2 * hd), BF16)],
        compiler_params=_cparams("parallel", "parallel"),
        name="fox_attention",
    )(proj, proj, proj, proj, qn, kn, pieces, pieces_meta)


def _gate_rows(g, seq, n_meta, block):
    bsz, _, heads = g.shape
    gt = jnp.transpose(g, (0, 2, 1))
    real = gt[:, :, :seq].reshape(bsz, heads, seq // block, block)
    meta = gt[:, :, seq:].reshape(bsz, heads, 1, n_meta)
    return real, meta


def kernel(x, meta_tokens, a_norm, a_w_in, a_conv_w, a_conv_b, a_w_q, a_w_k, a_w_v, a_w_gate, a_b_gate,
           a_head_norm, a_skip, a_w_out, b_norm, b_w_in, b_b_f, b_q_norm, b_k_norm, b_w_out, final_norm):
    bsz, seq, d = x.shape
    n_meta = meta_tokens.shape[0]
    total = seq + n_meta
    m = bsz * total
    depth_a = a_norm.shape[0]
    depth_b = b_norm.shape[0]
    assert depth_a == 1 and depth_b == 1, "layer schedule: one mLSTM layer followed by one FoX layer"
    tm_in = _row_tile(m, IN_PROJ_ROWS)
    tm_out = _row_tile(m, OUT_PROJ_ROWS)

    meta = jnp.broadcast_to(meta_tokens[None].astype(x.dtype), (bsz, n_meta, d))
    h0 = jnp.concatenate([x, meta], axis=1).reshape(m, d)

    heads = A_HEADS
    e = a_w_out.shape[1]
    dqk = a_w_q.shape[3]
    dv = a_w_v.shape[3]
    nqk = heads * dqk
    w_gate = a_w_gate[0].astype(BF16)
    gq = w_gate[:nqk].reshape(heads, dqk, 2 * heads)
    gk = w_gate[nqk:2 * nqk].reshape(heads, dqk, 2 * heads)
    gv = w_gate[2 * nqk:].reshape(heads, dv, 2 * heads)

    inner = b_w_out.shape[1]
    bheads = inner // B_HEAD_DIM
    proj, w_main_t, w_f_t, a_w_out16, b_w_out16 = _norm_matmul_prep(
        h0, a_norm[0][None], a_w_in[0].astype(BF16), b_w_in[0].T, a_w_out[0], b_w_out[0],
        n_qkv=3 * inner, n_f=bheads, tm=tm_in, tn=IN_PROJ_COLS, name="mlstm_in_proj")
    proj = proj.reshape(bsz, total, 3 * e)
    xc, q, k, v, gates = _conv_qkv(
        proj, a_conv_w[0], a_conv_b[0][None], a_w_q[0].astype(BF16), a_w_k[0].astype(BF16),
        a_w_v[0].astype(BF16), gq, gk, gv, a_b_gate[0][None], seq=seq, n_meta=n_meta)
    ig_real, ig_meta = _gate_rows(gates[..., :heads], seq, n_meta, MLSTM_CHUNK)
    f_real, f_meta = _gate_rows(gates[..., heads:], seq, n_meta, MLSTM_CHUNK)
    hs = _mlstm(q, k, v, ig_real, f_real, ig_meta, f_meta, seq=seq, n_meta=n_meta)
    h1 = _gated_out_proj(hs.reshape(m, e), proj.reshape(m, 3 * e), xc.reshape(m, e), a_head_norm[0][None],
                         a_skip[0][None], a_w_out16, h0, tm=tm_out, heads_per_step=OUT_PROJ_HEADS,
                         name="mlstm_out_proj")

    projb, f_pre = _norm_matmul_extra(h1, b_norm[0][None], w_main_t, w_f_t, tm=tm_in, tn=IN_PROJ_COLS,
                                      name="fox_in_proj")
    fr, fm = _gate_rows(f_pre[:, :bheads].reshape(bsz, total, bheads), seq, n_meta, FOX_QBLOCK)
    fm = jnp.pad(fm.reshape(bsz, bheads, n_meta), ((0, 0), (0, 0), (0, V7X_LANES - n_meta)))
    pieces, pieces_meta = _fox_gates(b_b_f[0], fr, fm, n_meta=n_meta)
    yb = _fox(projb.reshape(bsz, total, 4 * inner), b_q_norm[0][None], b_k_norm[0][None],
              pieces, pieces_meta, seq=seq, n_meta=n_meta)
    return _matmul_residual_norm(yb, b_w_out16, h1.reshape(bsz, total, d), final_norm[None],
                                 seq=seq, tm=_row_tile(seq, OUT_PROJ_ROWS), name="fox_out_proj_norm")
```
